```python
import math
import jax
import jax.numpy as jnp
from jax import lax
import numpy as np

D_MODEL = 2048
BATCH = 2
SEQ = 4096
DEPTH = 4
DEC_BATCH = 8
DEC_SEQ = 64
PAST_LEN = 1024

CHUNK = 64
Q_BLOCK = 128
N_MIXERS = 4
MIX_WIDTH = 1536
MEM_TOKENS = 256
MEM_HEADS = 4
MEM_HEAD_DIM = 128
MEM_WIDTH = MEM_HEADS * MEM_HEAD_DIM
OUT_WIDTH = MIX_WIDTH + MEM_WIDTH
ROPE_THETA = 500000.0

A_HEADS = 6
A_QK_DIM = 128
A_V_DIM = 2 * A_QK_DIM
A_ROT_DIM = A_QK_DIM // 4
A_QK_WIDTH = A_HEADS * 2 * A_QK_DIM
A_V_WIDTH = A_HEADS * A_V_DIM

B_HEADS = 12
B_HEAD_DIM = 128
B_WIDTH = B_HEADS * B_HEAD_DIM
B_LEFT_CHUNKS = 8
B_WINDOW = B_LEFT_CHUNKS * CHUNK
B_REL_CLIP = 128

C_HEADS = 12
C_Q_RANK = 768
C_KV_RANK = 512
C_NOPE_DIM = 128
C_ROPE_DIM = 64
C_V_DIM = 128
C_ROPE_THETA = 10000.0

D_HEADS = 12
D_HEAD_DIM = 128
D_WIDTH = D_HEADS * D_HEAD_DIM

D_FF = 5632
CONV_WIDTH = 3

IN_WIDTH_A = 2 * A_QK_WIDTH + A_V_WIDTH + MEM_WIDTH
IN_WIDTH_B = 3 * B_WIDTH + MEM_WIDTH
IN_WIDTH_C = C_Q_RANK + C_KV_RANK + C_ROPE_DIM + MEM_WIDTH
IN_WIDTH_D = 3 * D_WIDTH + MEM_WIDTH

N_LAYERS_A = (DEPTH + N_MIXERS - 1) // N_MIXERS
N_LAYERS_B = (DEPTH + N_MIXERS - 2) // N_MIXERS
N_LAYERS_C = (DEPTH + N_MIXERS - 3) // N_MIXERS
N_LAYERS_D = (DEPTH + N_MIXERS - 4) // N_MIXERS

DEEPNORM_ALPHA = (2 * DEPTH) ** 0.25
DEEPNORM_BETA = (8 * DEPTH) ** -0.25
NORM_EPS = 1e-5
NEG_INF = -1e30

kernel_name = "hybrid_streaming_encoder_step"


def layer_norm(x, g, b):
    xf = x.astype(jnp.float32)
    mu = jnp.mean(xf, axis=-1, keepdims=True)
    var = jnp.mean(jnp.square(xf - mu), axis=-1, keepdims=True)
    y = (xf - mu) * lax.rsqrt(var + NORM_EPS) * g.astype(jnp.float32) + b.astype(jnp.float32)
    return y.astype(x.dtype)


def rms_norm(x, g):
    xf = x.astype(jnp.float32)
    y = xf * lax.rsqrt(jnp.mean(jnp.square(xf), axis=-1, keepdims=True) + NORM_EPS) * g.astype(jnp.float32)
    return y.astype(x.dtype)


def rope(x, pos, rot_dim, theta):
    half = rot_dim // 2
    inv_freq = theta ** (-jnp.arange(half, dtype=jnp.float32) / half)
    ang = pos.astype(jnp.float32)[:, None] * inv_freq[None, :]
    cos = jnp.cos(ang)[:, None, :].astype(x.dtype)
    sin = jnp.sin(ang)[:, None, :].astype(x.dtype)
    x1, x2, rest = x[..., :half], x[..., half:rot_dim], x[..., rot_dim:]
    return jnp.concatenate([x1 * cos - x2 * sin, x2 * cos + x1 * sin, rest], axis=-1)


def chunk_mask(q_pos, k_pos):
    return (k_pos[None, :] // CHUNK) <= (q_pos[:, None] // CHUNK)


def sweep_queries(block_fn, q, q_pos):
    bsz, t = q.shape[0], q.shape[1]
    if t <= Q_BLOCK or t % Q_BLOCK:
        return block_fn(q, q_pos)
    nb = t // Q_BLOCK
    qb = jnp.moveaxis(q.reshape(bsz, nb, Q_BLOCK, *q.shape[2:]), 1, 0)
    out = lax.map(lambda a: block_fn(a[0], a[1]), (qb, q_pos.reshape(nb, Q_BLOCK)))
    return jnp.moveaxis(out, 0, 1).reshape(bsz, t, *out.shape[3:])


def diff_lambda_init(layer_idx):
    return 0.8 - 0.6 * math.exp(-0.3 * layer_idx)


def mixer_diff(z, pos, past, lam_q1, lam_k1, lam_q2, lam_k2, head_norm_g, lam_init):
    bsz, t, _ = z.shape
    q, k, v = jnp.split(z, [A_QK_WIDTH, 2 * A_QK_WIDTH], axis=-1)
    q = rope(q.reshape(bsz, t, 2 * A_HEADS, A_QK_DIM), pos, A_ROT_DIM, ROPE_THETA)
    k = rope(k.reshape(bsz, t, 2 * A_HEADS, A_QK_DIM), pos, A_ROT_DIM, ROPE_THETA)
    q = q.reshape(bsz, t, A_HEADS, 2, A_QK_DIM)
    k_rows = k.reshape(bsz, t, A_HEADS, 2 * A_QK_DIM)
    v_rows = v.reshape(bsz, t, A_HEADS, A_V_DIM)
    if past is None:
        k_all, v_all = k_rows, v_rows
    else:
        k_all = jnp.concatenate([past[0], k_rows], axis=1)
        v_all = jnp.concatenate([past[1], v_rows], axis=1)
    k_pos = jnp.arange(k_all.shape[1])
    k_all = k_all.reshape(bsz, -1, A_HEADS, 2, A_QK_DIM)
    f32 = jnp.float32
    lam = (jnp.exp(jnp.sum(lam_q1.astype(f32) * lam_k1.astype(f32)))
           - jnp.exp(jnp.sum(lam_q2.astype(f32) * lam_k2.astype(f32))) + lam_init)
    scale = A_QK_DIM ** -0.5

    def block(qb, pb):
        s = jnp.einsum('bqhmd,bkhmd->bhmqk', qb, k_all).astype(f32) * scale
        p = jax.nn.softmax(jnp.where(chunk_mask(pb, k_pos)[None, None, None], s, NEG_INF), axis=-1)
        w = p[:, :, 0] - lam * p[:, :, 1]
        return jnp.einsum('bhqk,bkhd->bqhd', w.astype(v_all.dtype), v_all)

    o = sweep_queries(block, q, pos)
    o = rms_norm(o, head_norm_g) * (1.0 - lam_init)
    return o.reshape(bsz, t, A_V_WIDTH), (k_rows, v_rows)


def rel_bias_lookup(rel_bias, rel):
    return rel_bias[:, jnp.clip(rel, -B_REL_CLIP, B_REL_CLIP) + B_REL_CLIP]


def band_core(q, k, v, bias, mask):
    s = jnp.einsum('bnqhd,bnkhd->bnhqk', q, k).astype(jnp.float32) * (B_HEAD_DIM ** -0.5)
    s = s + bias[None, None].astype(jnp.float32)
    p = jax.nn.softmax(jnp.where(mask[None, :, None], s, NEG_INF), axis=-1)
    return jnp.einsum('bnhqk,bnkhd->bnqhd', p.astype(v.dtype), v)


def mixer_band(z, pos, past, rel_bias):
    bsz, t, _ = z.shape
    q, k, v = [a.reshape(bsz, t, B_HEADS, B_HEAD_DIM) for a in jnp.split(z, [B_WIDTH, 2 * B_WIDTH], axis=-1)]
    if past is None:
        nch = t // CHUNK
        band = (B_LEFT_CHUNKS + 1) * CHUNK
        padw = ((0, 0), (B_WINDOW, 0), (0, 0), (0, 0))
        kp = jnp.pad(k, padw).reshape(bsz, nch + B_LEFT_CHUNKS, CHUNK, B_HEADS, B_HEAD_DIM)
        vp = jnp.pad(v, padw).reshape(bsz, nch + B_LEFT_CHUNKS, CHUNK, B_HEADS, B_HEAD_DIM)
        idx = jnp.arange(nch)[:, None] + jnp.arange(B_LEFT_CHUNKS + 1)[None, :]
        kb = kp[:, idx].reshape(bsz, nch, band, B_HEADS, B_HEAD_DIM)
        vb = vp[:, idx].reshape(bsz, nch, band, B_HEADS, B_HEAD_DIM)
        q_rel = jnp.arange(CHUNK)
        k_rel = jnp.arange(band) - B_WINDOW
        bias = rel_bias_lookup(rel_bias, q_rel[:, None] - k_rel[None, :])
        key_chunk = jnp.arange(nch)[:, None] + (k_rel // CHUNK)[None, :]
        mask = (key_chunk >= 0)[:, None, :]
        o = band_core(q.reshape(bsz, nch, CHUNK, B_HEADS, B_HEAD_DIM), kb, vb, bias, mask)
        keep = min(B_WINDOW, t)
        new_state = (k[:, t - keep:], v[:, t - keep:])
    else:
        buf_len = past[0].shape[1]
        k_all = jnp.concatenate([past[0], k], axis=1)
        v_all = jnp.concatenate([past[1], v], axis=1)
        k_pos = pos[0] - buf_len + jnp.arange(buf_len + t)
        bias = rel_bias_lookup(rel_bias, pos[:, None] - k_pos[None, :])
        qc, kc = pos[:, None] // CHUNK, k_pos[None, :] // CHUNK
        mask = ((kc <= qc) & (kc >= qc - B_LEFT_CHUNKS))[None]
        o = band_core(q[:, None], k_all[:, None], v_all[:, None], bias, mask)
        new_state = (k_all[:, t:], v_all[:, t:])
    return o.reshape(bsz, t, B_WIDTH), new_state


def mixer_mla(z, pos, past, q_norm_g, kv_norm_g, w_uq, w_ukv):
    bsz, t, _ = z.shape
    c_q, c_kv, k_rope = jnp.split(z, [C_Q_RANK, C_Q_RANK + C_KV_RANK], axis=-1)
    q = (rms_norm(c_q, q_norm_g) @ w_uq).reshape(bsz, t, C_HEADS, C_NOPE_DIM + C_ROPE_DIM)
    q = jnp.concatenate([q[..., :C_NOPE_DIM], rope(q[..., C_NOPE_DIM:], pos, C_ROPE_DIM, C_ROPE_THETA)], axis=-1)
    latent = rms_norm(c_kv, kv_norm_g)
    kr = rope(k_rope[:, :, None, :], pos, C_ROPE_DIM, C_ROPE_THETA)[:, :, 0, :]
    if past is None:
        lat_all, kr_all = latent, kr
    else:
        lat_all = jnp.concatenate([past[0], latent], axis=1)
        kr_all = jnp.concatenate([past[1], kr], axis=1)
    s_len = lat_all.shape[1]
    kv = (lat_all @ w_ukv).reshape(bsz, s_len, C_HEADS, C_NOPE_DIM + C_V_DIM)
    k_nope, v = kv[..., :C_NOPE_DIM], kv[..., C_NOPE_DIM:]
    k_pos = jnp.arange(s_len)
    scale = (C_NOPE_DIM + C_ROPE_DIM) ** -0.5

    def block(qb, pb):
        s = (jnp.einsum('bqhd,bkhd->bhqk', qb[..., :C_NOPE_DIM], k_nope)
             + jnp.einsum('bqhd,bkd->bhqk', qb[..., C_NOPE_DIM:], kr_all)).astype(jnp.float32) * scale
        p = jax.nn.softmax(jnp.where(chunk_mask(pb, k_pos)[None, None], s, NEG_INF), axis=-1)
        return jnp.einsum('bhqk,bkhd->bqhd', p.astype(v.dtype), v)

    o = sweep_queries(block, q, pos)
    return o.reshape(bsz, t, C_HEADS * C_V_DIM), (latent, kr)


def mixer_stick(z, pos, past):
    bsz, t, _ = z.shape
    q, k, v = [a.reshape(bsz, t, D_HEADS, D_HEAD_DIM) for a in jnp.split(z, [D_WIDTH, 2 * D_WIDTH], axis=-1)]
    if past is None:
        k_all, v_all = k, v
    else:
        k_all = jnp.concatenate([past[0], k], axis=1)
        v_all = jnp.concatenate([past[1], v], axis=1)
    k_pos = jnp.arange(k_all.shape[1])
    scale = D_HEAD_DIM ** -0.5

    def block(qb, pb):
        logits = jnp.einsum('bqhd,bkhd->bhqk', qb, k_all).astype(jnp.float32) * scale
        allowed = (k_pos[None, :] < pb[:, None])[None, None]
        log_1m_beta = jnp.where(allowed, jax.nn.log_sigmoid(-logits), 0.0)
        tail = lax.cumsum(log_1m_beta, axis=3, reverse=True) - log_1m_beta
        a = jnp.where(allowed, jnp.exp(jax.nn.log_sigmoid(logits) + tail), 0.0)
        return jnp.einsum('bhqk,bkhd->bqhd', a.astype(v_all.dtype), v_all)

    o = sweep_queries(block, q, pos)
    return o.reshape(bsz, t, D_WIDTH), (k, v)


def project_memory(mem, w_mem_kv):
    bsz, n, _ = mem.shape
    kv = mem @ w_mem_kv
    return (kv[..., :MEM_WIDTH].reshape(bsz, n, MEM_HEADS, MEM_HEAD_DIM),
            kv[..., MEM_WIDTH:].reshape(bsz, n, MEM_HEADS, MEM_HEAD_DIM))


def memory_attention(q_mem, mem_k, mem_v):
    bsz, t, _ = q_mem.shape
    q = q_mem.reshape(bsz, t, MEM_HEADS, MEM_HEAD_DIM)
    s = jnp.einsum('bqhd,bkhd->bhqk', q, mem_k).astype(jnp.float32) * (MEM_HEAD_DIM ** -0.5)
    p = jax.nn.softmax(s, axis=-1)
    return jnp.einsum('bhqk,bkhd->bqhd', p.astype(mem_v.dtype), mem_v).reshape(bsz, t, MEM_WIDTH)


def conv_ffn(x, conv_state, w_up, conv_w, conv_b, w_down):
    t = x.shape[1]
    u = x @ w_up
    u_ext = jnp.concatenate([conv_state.astype(u.dtype), u], axis=1)
    h = conv_b
    for j in range(CONV_WIDTH):
        h = h + conv_w[j] * u_ext[:, j:j + t]
    gate, val = jnp.split(h, 2, axis=-1)
    return (jax.nn.silu(gate) * val) @ w_down, u_ext[:, t:]


def trunk_layer(i, x, pos, past_mix, mem_k, mem_v, conv_state, mix_params,
                w_in, w_o, ln1_g, ln1_b, w_up, conv_w, conv_b, w_down, ln2_g, ln2_b):
    m = i % N_MIXERS
    z = x @ w_in
    z_mix, q_mem = z[..., :-MEM_WIDTH], z[..., -MEM_WIDTH:]
    if m == 0:
        o_mix, mix_state = mixer_diff(z_mix, pos, past_mix, *mix_params, lam_init=diff_lambda_init(i))
    elif m == 1:
        o_mix, mix_state = mixer_band(z_mix, pos, past_mix, *mix_params)
    elif m == 2:
        o_mix, mix_state = mixer_mla(z_mix, pos, past_mix, *mix_params)
    else:
        o_mix, mix_state = mixer_stick(z_mix, pos, past_mix)
    o = jnp.concatenate([o_mix, memory_attention(q_mem, mem_k, mem_v)], axis=-1) @ w_o
    x = layer_norm(DEEPNORM_ALPHA * x + o, ln1_g, ln1_b)
    f, conv_new = conv_ffn(x, conv_state, w_up, conv_w, conv_b, w_down)
    x = layer_norm(DEEPNORM_ALPHA * x + f, ln2_g, ln2_b)
    return x, mix_state, conv_new


def setup_inputs(seed: int = 0) -> dict:
    keys = iter(jax.random.split(jax.random.key(seed), 48))

    def normal(shape, scale=1.0):
        return jax.random.normal(next(keys), shape, jnp.float32) * scale

    def gain(shape):
        return 1.0 + normal(shape, 0.02)

    b_rows = min(B_WINDOW, PAST_LEN)
    return {
        "x_prompt": normal((BATCH, SEQ, D_MODEL)),
        "x_sample": normal((DEC_BATCH, DEC_SEQ, D_MODEL)),
        "mem_prompt": normal((BATCH, MEM_TOKENS, D_MODEL)),
        "cache_a_k": normal((N_LAYERS_A, DEC_BATCH, PAST_LEN, A_HEADS, 2 * A_QK_DIM)),
        "cache_a_v": normal((N_LAYERS_A, DEC_BATCH, PAST_LEN, A_HEADS, A_V_DIM)),
        "cache_b_k": normal((N_LAYERS_B, DEC_BATCH, b_rows, B_HEADS, B_HEAD_DIM)),
        "cache_b_v": normal((N_LAYERS_B, DEC_BATCH, b_rows, B_HEADS, B_HEAD_DIM)),
        "cache_c_latent": normal((N_LAYERS_C, DEC_BATCH, PAST_LEN, C_KV_RANK)),
        "cache_c_krope": normal((N_LAYERS_C, DEC_BATCH, PAST_LEN, C_ROPE_DIM)),
        "cache_d_k": normal((N_LAYERS_D, DEC_BATCH, PAST_LEN, D_HEADS, D_HEAD_DIM)),
        "cache_d_v": normal((N_LAYERS_D, DEC_BATCH, PAST_LEN, D_HEADS, D_HEAD_DIM)),
        "cache_mem_k": normal((DEPTH, DEC_BATCH, MEM_TOKENS, MEM_HEADS, MEM_HEAD_DIM)),
        "cache_mem_v": normal((DEPTH, DEC_BATCH, MEM_TOKENS, MEM_HEADS, MEM_HEAD_DIM)),
        "state_ffn_conv": normal((DEPTH, DEC_BATCH, CONV_WIDTH - 1, 2 * D_FF)),
        "w_in_a": normal((N_LAYERS_A, D_MODEL, IN_WIDTH_A), D_MODEL ** -0.5),
        "w_in_b": normal((N_LAYERS_B, D_MODEL, IN_WIDTH_B), D_MODEL ** -0.5),
        "w_in_c": normal((N_LAYERS_C, D_MODEL, IN_WIDTH_C), D_MODEL ** -0.5),
        "w_in_d": normal((N_LAYERS_D, D_MODEL, IN_WIDTH_D), D_MODEL ** -0.5),
        "diff_lambda_q1": normal((N_LAYERS_A, A_QK_DIM), 0.1),
        "diff_lambda_k1": normal((N_LAYERS_A, A_QK_DIM), 0.1),
        "diff_lambda_q2": normal((N_LAYERS_A, A_QK_DIM), 0.1),
        "diff_lambda_k2": normal((N_LAYERS_A, A_QK_DIM), 0.1),
        "diff_norm_g": gain((N_LAYERS_A, A_V_DIM)),
        "band_rel_bias": normal((N_LAYERS_B, B_HEADS, 2 * B_REL_CLIP + 1), 0.1),
        "mla_q_norm_g": gain((N_LAYERS_C, C_Q_RANK)),
        "mla_kv_norm_g": gain((N_LAYERS_C, C_KV_RANK)),
        "mla_w_uq": normal((N_LAYERS_C, C_Q_RANK, C_HEADS * (C_NOPE_DIM + C_ROPE_DIM)), C_Q_RANK ** -0.5),
        "mla_w_ukv": normal((N_LAYERS_C, C_KV_RANK, C_HEADS * (C_NOPE_DIM + C_V_DIM)), C_KV_RANK ** -0.5),
        "w_mem_kv": normal((DEPTH, D_MODEL, 2 * MEM_WIDTH), D_MODEL ** -0.5),
        "w_o": normal((DEPTH, OUT_WIDTH, D_MODEL), DEEPNORM_BETA * OUT_WIDTH ** -0.5),
        "ln1_g": gain((DEPTH, D_MODEL)),
        "ln1_b": normal((DEPTH, D_MODEL), 0.02),
        "w_up": normal((DEPTH, D_MODEL, 2 * D_FF), D_MODEL ** -0.5),
        "conv_ffn_w": normal((DEPTH, CONV_WIDTH, 2 * D_FF), CONV_WIDTH ** -0.5),
        "conv_ffn_b": normal((DEPTH, 2 * D_FF), 0.02),
        "w_down": normal((DEPTH, D_FF, D_MODEL), DEEPNORM_BETA * D_FF ** -0.5),
        "ln2_g": gain((DEPTH, D_MODEL)),
        "ln2_b": normal((DEPTH, D_MODEL), 0.02),
    }


def reference(x_prompt, x_sample, mem_prompt,
              cache_a_k, cache_a_v, cache_b_k, cache_b_v, cache_c_latent, cache_c_krope,
              cache_d_k, cache_d_v, cache_mem_k, cache_mem_v, state_ffn_conv,
              w_in_a, w_in_b, w_in_c, w_in_d,
              diff_lambda_q1, diff_lambda_k1, diff_lambda_q2, diff_lambda_k2, diff_norm_g,
              band_rel_bias, mla_q_norm_g, mla_kv_norm_g, mla_w_uq, mla_w_ukv,
              w_mem_kv, w_o, ln1_g, ln1_b, w_up, conv_ffn_w, conv_ffn_b, w_down, ln2_g, ln2_b):
    past_len = cache_d_k.shape[2]
    pos_p = jnp.arange(x_prompt.shape[1])
    pos_s = past_len + jnp.arange(x_sample.shape[1])
    w_in_by_type = (w_in_a, w_in_b, w_in_c, w_in_d)
    caches_by_type = ((cache_a_k, cache_a_v), (cache_b_k, cache_b_v),
                      (cache_c_latent, cache_c_krope), (cache_d_k, cache_d_v))
    params_by_type = ((diff_lambda_q1, diff_lambda_k1, diff_lambda_q2, diff_lambda_k2, diff_norm_g),
                      (band_rel_bias,),
                      (mla_q_norm_g, mla_kv_norm_g, mla_w_uq, mla_w_ukv),
                      ())
    states_p = [([], []) for _ in range(N_MIXERS)]
    states_s = [([], []) for _ in range(N_MIXERS)]
    mem_k_p, mem_v_p, conv_p, conv_s = [], [], [], []
    x_p, x_s = x_prompt, x_sample
    for i in range(DEPTH):
        m, j = i % N_MIXERS, i // N_MIXERS
        mix_params = tuple(p[j] for p in params_by_type[m])
        shared = (w_in_by_type[m][j], w_o[i], ln1_g[i], ln1_b[i], w_up[i], conv_ffn_w[i],
                  conv_ffn_b[i], w_down[i], ln2_g[i], ln2_b[i])
        mk, mv = project_memory(mem_prompt, w_mem_kv[i])
        conv0 = jnp.zeros((x_p.shape[0], CONV_WIDTH - 1, 2 * D_FF), x_p.dtype)
        x_p, st_p, cv_p = trunk_layer(i, x_p, pos_p, None, mk, mv, conv0, mix_params, *shared)
        past = (caches_by_type[m][0][j], caches_by_type[m][1][j])
        x_s, st_s, cv_s = trunk_layer(i, x_s, pos_s, past, cache_mem_k[i], cache_mem_v[i],
                                      state_ffn_conv[i], mix_params, *shared)
        for a in range(2):
            states_p[m][a].append(st_p[a])
            states_s[m][a].append(st_s[a])
        mem_k_p.append(mk)
        mem_v_p.append(mv)
        conv_p.append(cv_p)
        conv_s.append(cv_s)
    new_a_k_prompt, new_a_v_prompt = jnp.stack(states_p[0][0]), jnp.stack(states_p[0][1])
    new_b_k_prompt, new_b_v_prompt = jnp.stack(states_p[1][0]), jnp.stack(states_p[1][1])
    new_c_latent_prompt, new_c_krope_prompt = jnp.stack(states_p[2][0]), jnp.stack(states_p[2][1])
    new_d_k_prompt, new_d_v_prompt = jnp.stack(states_p[3][0]), jnp.stack(states_p[3][1])
    new_mem_k_prompt, new_mem_v_prompt = jnp.stack(mem_k_p), jnp.stack(mem_v_p)
    new_ffn_conv_prompt = jnp.stack(conv_p)
    new_a_k_sample, new_a_v_sample = jnp.stack(states_s[0][0]), jnp.stack(states_s[0][1])
    new_b_k_sample, new_b_v_sample = jnp.stack(states_s[1][0]), jnp.stack(states_s[1][1])
    new_c_latent_sample, new_c_krope_sample = jnp.stack(states_s[2][0]), jnp.stack(states_s[2][1])
    new_d_k_sample, new_d_v_sample = jnp.stack(states_s[3][0]), jnp.stack(states_s[3][1])
    new_ffn_conv_sample = jnp.stack(conv_s)
    return (x_p, x_s,
            new_a_k_prompt, new_a_v_prompt, new_b_k_prompt, new_b_v_prompt,
            new_c_latent_prompt, new_c_krope_prompt, new_d_k_prompt, new_d_v_prompt,
            new_mem_k_prompt, new_mem_v_prompt, new_ffn_conv_prompt,
            new_a_k_sample, new_a_v_sample, new_b_k_sample, new_b_v_sample,
            new_c_latent_sample, new_c_krope_sample, new_d_k_sample, new_d_v_sample,
            new_ffn_conv_sample)
```

```python
import functools
import math

import jax
import jax.numpy as jnp
from jax import lax
from jax.experimental import pallas as pl
from jax.experimental.pallas import tpu as pltpu

F32 = jnp.float32
BF16 = jnp.bfloat16

D_MODEL = 2048
DEPTH = 4
CHUNK = 64
CHUNK_SHIFT = 6
N_MIXERS = 4
MIX_WIDTH = 1536
MEM_HEADS = 4
MEM_WIDTH = 512
HEAD = 128
ROPE_THETA = 500000.0

A_HEADS = 6
A_QK_WIDTH = 1536
A_ROT_DIM = 32

B_HEADS = 12
B_WIDTH = 1536
B_WINDOW = 512
B_REL_CLIP = 128
BAND_TQ = 128
BAND_KEYS = B_WINDOW + BAND_TQ
BAND_TABLE = 768

C_HEADS = 12
C_Q_RANK = 768
C_KV_RANK = 512
C_NOPE_DIM = 128
C_ROPE_DIM = 64
C_ROPE_THETA = 10000.0
C_IN_PAD = 1920

D_HEADS = 12
D_WIDTH = 1536

D_FF = 5632
FFN_HALO = 16

DEEPNORM_ALPHA = (2 * DEPTH) ** 0.25
NORM_EPS = 1e-5
NEG_INF = -1e30

VMEM_LIMIT = 56 * 1024 * 1024


def _params(n_axes):
    return pltpu.CompilerParams(dimension_semantics=("arbitrary",) * n_axes,
                                vmem_limit_bytes=VMEM_LIMIT)


def _pick(n, cands):
    for c in cands:
        if n % c == 0:
            return c
    return n


def _mm_kernel(a_ref, b_ref, o_ref, abf_ref):
    @pl.when(pl.program_id(1) == 0)
    def _():
        abf_ref[...] = a_ref[...].astype(BF16)

    o_ref[...] = jnp.dot(abf_ref[...], b_ref[...], preferred_element_type=F32).astype(o_ref.dtype)


def matmul(a, b, *, tm, tn, out_dtype=F32):
    m, k = a.shape
    n = b.shape[1]
    assert m % tm == 0 and n % tn == 0, (a.shape, b.shape, tm, tn)
    return pl.pallas_call(
        _mm_kernel,
        grid=(m // tm, n // tn),
        in_specs=[pl.BlockSpec((tm, k), lambda i, j: (i, 0)),
                  pl.BlockSpec((k, tn), lambda i, j: (0, j))],
        out_specs=pl.BlockSpec((tm, tn), lambda i, j: (i, j)),
        out_shape=jax.ShapeDtypeStruct((m, n), out_dtype),
        scratch_shapes=[pltpu.VMEM((tm, k), BF16)],
        compiler_params=_params(2),
        name="matmul",
    )(a, b)


def _layer_norm_rows(y, g, b):
    mu = jnp.mean(y, axis=-1, keepdims=True)
    d = y - mu
    var = jnp.mean(d * d, axis=-1, keepdims=True)
    return d * lax.rsqrt(var + NORM_EPS) * g + b


def _out_proj_ln_kernel(a1_ref, a2_ref, w_ref, x_ref, g_ref, b_ref, o_ref):
    k1 = a1_ref.shape[1]
    f = jnp.dot(a1_ref[...], w_ref[:k1, :], preferred_element_type=F32)
    f = f + jnp.dot(a2_ref[...], w_ref[k1:, :], preferred_element_type=F32)
    y = DEEPNORM_ALPHA * x_ref[...] + f
    o_ref[...] = _layer_norm_rows(y, g_ref[...], b_ref[...])


def out_proj_ln(a1, a2, w, x, g, b, *, tm):
    m, k1 = a1.shape
    k2 = a2.shape[1]
    n = w.shape[1]
    row = lambda i: (i, 0)
    fixed = lambda i: (0, 0)
    return pl.pallas_call(
        _out_proj_ln_kernel,
        grid=(m // tm,),
        in_specs=[pl.BlockSpec((tm, k1), row), pl.BlockSpec((tm, k2), row),
                  pl.BlockSpec((k1 + k2, n), fixed), pl.BlockSpec((tm, n), row),
                  pl.BlockSpec((1, n), fixed), pl.BlockSpec((1, n), fixed)],
        out_specs=pl.BlockSpec((tm, n), row),
        out_shape=jax.ShapeDtypeStruct((m, n), F32),
        compiler_params=_params(1),
        name="out_proj_ln",
    )(a1, a2, w, x, g, b)


def _down_proj_ln_kernel(a_ref, w_ref, x_ref, g_ref, b_ref, o_ref, acc_ref):
    kk = pl.program_id(1)

    @pl.when(kk == 0)
    def _():
        acc_ref[...] = jnp.zeros_like(acc_ref)

    acc_ref[...] += jnp.dot(a_ref[...], w_ref[...], preferred_element_type=F32)

    @pl.when(kk == pl.num_programs(1) - 1)
    def _():
        y = DEEPNORM_ALPHA * x_ref[...] + acc_ref[...]
        o_ref[...] = _layer_norm_rows(y, g_ref[...], b_ref[...])


def down_proj_ln(a, w, x, g, b, *, tm, tk):
    m, k = a.shape
    n = w.shape[1]
    return pl.pallas_call(
        _down_proj_ln_kernel,
        grid=(m // tm, k // tk),
        in_specs=[pl.BlockSpec((tm, tk), lambda i, kk: (i, kk)),
                  pl.BlockSpec((tk, n), lambda i, kk: (kk, 0)),
                  pl.BlockSpec((tm, n), lambda i, kk: (i, 0)),
                  pl.BlockSpec((1, n), lambda i, kk: (0, 0)),
                  pl.BlockSpec((1, n), lambda i, kk: (0, 0))],
        out_specs=pl.BlockSpec((tm, n), lambda i, kk: (i, 0)),
        out_shape=jax.ShapeDtypeStruct((m, n), F32),
        scratch_shapes=[pltpu.VMEM((tm, n), F32)],
        compiler_params=_params(2),
        name="down_proj_ln",
    )(a, w, x, g, b)


def _silu(x):
    return x * (1.0 / (1.0 + jnp.exp(-x)))


def _ffn_up_kernel(x_ref, xh_ref, wg_ref, wv_ref, cwg_ref, cwv_ref, cbg_ref, cbv_ref,
                   sg_ref, sv_ref, o_ref, xcat_ref, *, tiles_per_seq):
    i = pl.program_id(0)
    j = pl.program_id(1)

    @pl.when(j == 0)
    def _():
        xcat_ref[:FFN_HALO, :] = xh_ref[...].astype(BF16)
        xcat_ref[FFN_HALO:, :] = x_ref[...].astype(BF16)

    xe = xcat_ref[...]
    ug = jnp.dot(xe, wg_ref[...], preferred_element_type=F32)
    uv = jnp.dot(xe, wv_ref[...], preferred_element_type=F32)

    def conv(u, cw_ref, cb_ref):
        cw = cw_ref[...]
        p1 = pltpu.roll(u, 1, 0)[FFN_HALO:]
        p2 = pltpu.roll(u, 2, 0)[FFN_HALO:]
        return cb_ref[...] + cw[0:1] * p2 + cw[1:2] * p1 + cw[2:3] * u[FFN_HALO:]

    o_ref[...] = (_silu(conv(ug, cwg_ref, cbg_ref)) * conv(uv, cwv_ref, cbv_ref)).astype(o_ref.dtype)

    @pl.when(i % tiles_per_seq == 0)
    def _():
        rid = lax.broadcasted_iota(jnp.int32, (FFN_HALO, o_ref.shape[1]), 0)

        def conv_head(u, cw_ref, cb_ref, st_ref):
            cw = cw_ref[...]
            cur = u[FFN_HALO:2 * FFN_HALO]
            st = st_ref[0]
            s0, s1 = st[0:1], st[1:2]
            p1 = jnp.where(rid == 0, s1, pltpu.roll(cur, 1, 0))
            p2 = jnp.where(rid == 0, s0, jnp.where(rid == 1, s1, pltpu.roll(cur, 2, 0)))
            return cb_ref[...] + cw[0:1] * p2 + cw[1:2] * p1 + cw[2:3] * cur

        hg = conv_head(ug, cwg_ref, cbg_ref, sg_ref)
        hv = conv_head(uv, cwv_ref, cbv_ref, sv_ref)
        o_ref[0:FFN_HALO, :] = (_silu(hg) * hv).astype(o_ref.dtype)


def ffn_up(x, w_up, conv_w, conv_b, state, *, seq_len, tm, tn):
    m, k = x.shape
    assert seq_len % tm == 0 and tm % FFN_HALO == 0 and D_FF % tn == 0
    tiles_per_seq = seq_len // tm
    nf = D_FF // tn
    halo_blocks = tm // FFN_HALO
    kern = functools.partial(_ffn_up_kernel, tiles_per_seq=tiles_per_seq)
    return pl.pallas_call(
        kern,
        grid=(m // tm, nf),
        in_specs=[
            pl.BlockSpec((tm, k), lambda i, j: (i, 0)),
            pl.BlockSpec((FFN_HALO, k), lambda i, j: (jnp.maximum(i * halo_blocks - 1, 0), 0)),
            pl.BlockSpec((k, tn), lambda i, j: (0, j)),
            pl.BlockSpec((k, tn), lambda i, j: (0, j + nf)),
            pl.BlockSpec((3, tn), lambda i, j: (0, j)),
            pl.BlockSpec((3, tn), lambda i, j: (0, j + nf)),
            pl.BlockSpec((1, tn), lambda i, j: (0, j)),
            pl.BlockSpec((1, tn), lambda i, j: (0, j + nf)),
            pl.BlockSpec((1, 2, tn), lambda i, j: (i // tiles_per_seq, 0, j)),
            pl.BlockSpec((1, 2, tn), lambda i, j: (i // tiles_per_seq, 0, j + nf)),
        ],
        out_specs=pl.BlockSpec((tm, tn), lambda i, j: (i, j)),
        out_shape=jax.ShapeDtypeStruct((m, D_FF), BF16),
        scratch_shapes=[pltpu.VMEM((tm + FFN_HALO, k), BF16)],
        compiler_params=_params(2),
        name="ffn_up",
    )(x, x, w_up, w_up, conv_w, conv_w, conv_b, conv_b, state, state)


def _rope_tables(pos, half, theta):
    inv_freq = theta ** (-jnp.arange(half, dtype=F32) / half)
    ang = pos.astype(F32)[:, None] * inv_freq[None, :]
    cos, sin = jnp.cos(ang), jnp.sin(ang)
    t = pos.shape[0]
    rest = HEAD - 2 * half
    cos_t = jnp.concatenate([cos, cos, jnp.ones((t, rest), F32)], axis=1)
    sin_up = jnp.concatenate([jnp.zeros((t, half), F32), sin, jnp.zeros((t, rest), F32)], axis=1)
    sin_dn = jnp.concatenate([-sin, jnp.zeros((t, half + rest), F32)], axis=1)
    return cos_t, sin_up, sin_dn


def _rope(x, cos_t, sin_up, sin_dn, half):
    return x * cos_t + pltpu.roll(x, half, 1) * sin_up + pltpu.roll(x, HEAD - half, 1) * sin_dn


def _rope_heads_kernel(x_ref, c_ref, su_ref, sd_ref, o_ref, *, half):
    o_ref[0] = _rope(x_ref[0], c_ref[...], su_ref[...], sd_ref[...], half)


def rope_heads(z, col0, n_heads, tables, half, *, tt):
    bsz, t, _ = z.shape
    c0 = col0 // HEAD
    tab = pl.BlockSpec((tt, HEAD), lambda b, i, h: (i, 0))
    return pl.pallas_call(
        functools.partial(_rope_heads_kernel, half=half),
        grid=(bsz, t // tt, n_heads),
        in_specs=[pl.BlockSpec((1, tt, HEAD), lambda b, i, h: (b, i, c0 + h)), tab, tab, tab],
        out_specs=pl.BlockSpec((1, tt, HEAD), lambda b, i, h: (b, i, h)),
        out_shape=jax.ShapeDtypeStruct((bsz, t, n_heads * HEAD), F32),
        compiler_params=_params(3),
        name="rope_heads",
    )(z, *tables)


def _rms_rows(x, g):
    return x * lax.rsqrt(jnp.mean(x * x, axis=-1, keepdims=True) + NORM_EPS) * g


def _mla_prep_kernel(z_ref, gq_ref, gkv_ref, c_ref, su_ref, sd_ref, cq_ref, lat_ref, kr_ref):
    z = z_ref[...]
    cq_ref[...] = _rms_rows(z[:, :C_Q_RANK], gq_ref[...]).astype(cq_ref.dtype)
    lat_ref[...] = _rms_rows(z[:, C_Q_RANK:C_Q_RANK + C_KV_RANK], gkv_ref[...])
    kr = z[:, C_IN_PAD - HEAD:]
    kr_ref[...] = _rope(kr, c_ref[...], su_ref[...], sd_ref[...], C_ROPE_DIM // 2)


def mla_prep(z2d, gq, gkv, tables, *, seq_len, tt):
    m = z2d.shape[0]
    nt = seq_len // tt
    row = lambda i: (i, 0)
    fixed = lambda i: (0, 0)
    tab = pl.BlockSpec((tt, HEAD), lambda i: (i % nt, 0))
    return pl.pallas_call(
        _mla_prep_kernel,
        grid=(m // tt,),
        in_specs=[pl.BlockSpec((tt, C_IN_PAD), row), pl.BlockSpec((1, C_Q_RANK), fixed),
                  pl.BlockSpec((1, C_KV_RANK), fixed), tab, tab, tab],
        out_specs=[pl.BlockSpec((tt, C_Q_RANK), row), pl.BlockSpec((tt, C_KV_RANK), row),
                   pl.BlockSpec((tt, HEAD), row)],
        out_shape=[jax.ShapeDtypeStruct((m, C_Q_RANK), BF16),
                   jax.ShapeDtypeStruct((m, C_KV_RANK), F32),
                   jax.ShapeDtypeStruct((m, HEAD), F32)],
        compiler_params=_params(1),
        name="mla_prep",
    )(z2d, gq, gkv, *tables)


def _nt_dot(a, b):
    return lax.dot_general(a, b, (((1,), (1,)), ((), ())), preferred_element_type=F32)


def _chunk_mask(tq, tk, q_start, k_start):
    qpos = q_start + lax.broadcasted_iota(jnp.int32, (tq, tk), 0)
    kpos = k_start + lax.broadcasted_iota(jnp.int32, (tq, tk), 1)
    return jnp.right_shift(kpos, CHUNK_SHIFT) <= jnp.right_shift(qpos, CHUNK_SHIFT)


def _softmax_step(s, v, m_ref, l_ref, acc_ref):
    m_old = m_ref[...]
    m_new = jnp.maximum(m_old, jnp.max(s, axis=-1, keepdims=True))
    p = jnp.exp(s - m_new)
    alpha = jnp.exp(m_old - m_new)
    l_ref[...] = alpha * l_ref[...] + jnp.sum(p, axis=-1, keepdims=True)
    acc_ref[...] = alpha * acc_ref[...] + jnp.dot(p.astype(BF16), v, preferred_element_type=F32)
    m_ref[...] = m_new


def _init_softmax(m_ref, l_ref, acc_ref):
    m_ref[...] = jnp.full_like(m_ref, NEG_INF)
    l_ref[...] = jnp.zeros_like(l_ref)
    acc_ref[...] = jnp.zeros_like(acc_ref)


def _n_key_blocks(i, tq, tk, q_off):
    return (q_off + (i + 1) * tq - 1) // tk + 1


def _diff_attn_kernel(q_ref, k_ref, v_ref, c_ref, su_ref, sd_ref, lam_ref, g_ref, o_ref,
                      m0, l0, a0, m1, l1, a1, *, tq, tk, q_off, lam_init):
    i = pl.program_id(2)
    nblk = _n_key_blocks(i, tq, tk, q_off)
    scale = HEAD ** -0.5
    tabs = (c_ref[...], su_ref[...], sd_ref[...])
    q = q_ref[0]
    qs = [(_rope(q[:, mm * HEAD:(mm + 1) * HEAD], *tabs, A_ROT_DIM // 2) * scale).astype(BF16)
          for mm in range(2)]
    _init_softmax(m0, l0, a0)
    _init_softmax(m1, l1, a1)
    stats = ((m0, l0, a0), (m1, l1, a1))

    def block(j, masked):
        k = k_ref[0, pl.ds(pl.multiple_of(j * tk, tk), tk), :].astype(BF16)
        v = v_ref[0, pl.ds(pl.multiple_of(j * tk, tk), tk), :].astype(BF16)
        for mm in range(2):
            s = _nt_dot(qs[mm], k[:, mm * HEAD:(mm + 1) * HEAD])
            if masked:
                s = jnp.where(_chunk_mask(tq, tk, q_off + i * tq, j * tk), s, NEG_INF)
            _softmax_step(s, v, *stats[mm])

    def body(j, carry):
        block(j, False)
        return carry

    lax.fori_loop(0, nblk - 1, body, 0)
    block(nblk - 1, True)

    lam_v = lam_ref[...]
    dots = jnp.sum(lam_v[0:2] * lam_v[2:4], axis=-1, keepdims=True)
    lam = jnp.exp(dots[0:1]) - jnp.exp(dots[1:2]) + lam_init
    o = a0[...] / l0[...] - lam * (a1[...] / l1[...])
    o_ref[0] = (_rms_rows(o, g_ref[...]) * (1.0 - lam_init)).astype(o_ref.dtype)


def diff_attention(q_src, q_col0, k_rows, v_rows, q_tables, lam_vecs, norm_g, *, tq, tk, q_off, lam_init):
    bsz, t, _ = q_src.shape
    s_len = k_rows.shape[1]
    assert s_len % tk == 0 and t % tq == 0
    c0 = q_col0 // (2 * HEAD)
    tab = pl.BlockSpec((tq, HEAD), lambda b, h, i: (i, 0))
    kern = functools.partial(_diff_attn_kernel, tq=tq, tk=tk, q_off=q_off, lam_init=lam_init)
    stat = [pltpu.VMEM((tq, 1), F32), pltpu.VMEM((tq, 1), F32), pltpu.VMEM((tq, 2 * HEAD), F32)]
    return pl.pallas_call(
        kern,
        grid=(bsz, A_HEADS, t // tq),
        in_specs=[pl.BlockSpec((1, tq, 2 * HEAD), lambda b, h, i: (b, i, c0 + h)),
                  pl.BlockSpec((1, s_len, 2 * HEAD), lambda b, h, i: (b, 0, h)),
                  pl.BlockSpec((1, s_len, 2 * HEAD), lambda b, h, i: (b, 0, h)),
                  tab, tab, tab,
                  pl.BlockSpec((4, HEAD), lambda b, h, i: (0, 0)),
                  pl.BlockSpec((1, 2 * HEAD), lambda b, h, i: (0, 0))],
        out_specs=pl.BlockSpec((1, tq, 2 * HEAD), lambda b, h, i: (b, i, h)),
        out_shape=jax.ShapeDtypeStruct((bsz, t, A_HEADS * 2 * HEAD), BF16),
        scratch_shapes=stat + stat,
        compiler_params=_params(3),
        name="diff_attention",
    )(q_src, k_rows, v_rows, *q_tables, lam_vecs, norm_g)


def _mla_attn_kernel(qn_ref, qr_ref, kn_ref, kr_ref, v_ref, c_ref, su_ref, sd_ref, o_ref,
                     m_ref, l_ref, acc_ref, *, tq, tk, q_off):
    i = pl.program_id(2)
    nblk = _n_key_blocks(i, tq, tk, q_off)
    scale = (C_NOPE_DIM + C_ROPE_DIM) ** -0.5
    qn = (qn_ref[0] * scale).astype(BF16)
    qr = (_rope(qr_ref[0], c_ref[...], su_ref[...], sd_ref[...], C_ROPE_DIM // 2) * scale).astype(BF16)
    _init_softmax(m_ref, l_ref, acc_ref)

    def block(j, masked):
        rows = pl.ds(pl.multiple_of(j * tk, tk), tk)
        s = _nt_dot(qn, kn_ref[0, rows, :].astype(BF16)) + _nt_dot(qr, kr_ref[0, rows, :].astype(BF16))
        if masked:
            s = jnp.where(_chunk_mask(tq, tk, q_off + i * tq, j * tk), s, NEG_INF)
        _softmax_step(s, v_ref[0, rows, :].astype(BF16), m_ref, l_ref, acc_ref)

    def body(j, carry):
        block(j, False)
        return carry

    lax.fori_loop(0, nblk - 1, body, 0)
    block(nblk - 1, True)
    o_ref[0] = (acc_ref[...] / l_ref[...]).astype(o_ref.dtype)


def mla_attention(q, kv, kr, q_tables, *, tq, tk, q_off):
    bsz, t, _ = q.shape
    s_len = kv.shape[1]
    assert s_len % tk == 0 and t % tq == 0
    tab = pl.BlockSpec((tq, HEAD), lambda b, h, i: (i, 0))
    kern = functools.partial(_mla_attn_kernel, tq=tq, tk=tk, q_off=q_off)
    return pl.pallas_call(
        kern,
        grid=(bsz, C_HEADS, t // tq),
        in_specs=[pl.BlockSpec((1, tq, HEAD), lambda b, h, i: (b, i, h)),
                  pl.BlockSpec((1, tq, HEAD), lambda b, h, i: (b, i, C_HEADS + h)),
                  pl.BlockSpec((1, s_len, HEAD), lambda b, h, i: (b, 0, 2 * h)),
                  pl.BlockSpec((1, s_len, HEAD), lambda b, h, i: (b, 0, 0)),
                  pl.BlockSpec((1, s_len, HEAD), lambda b, h, i: (b, 0, 2 * h + 1)),
                  tab, tab, tab],
        out_specs=pl.BlockSpec((1, tq, HEAD), lambda b, h, i: (b, i, h)),
        out_shape=jax.ShapeDtypeStruct((bsz, t, C_HEADS * HEAD), BF16),
        scratch_shapes=[pltpu.VMEM((tq, 1), F32), pltpu.VMEM((tq, 1), F32), pltpu.VMEM((tq, HEAD), F32)],
        compiler_params=_params(3),
        name="mla_attention",
    )(q, q, kv, kr, kv, *q_tables)


def _mem_attn_kernel(q_ref, k_ref, v_ref, o_ref):
    q = (q_ref[0] * (HEAD ** -0.5)).astype(BF16)
    s = _nt_dot(q, k_ref[0].astype(BF16))
    m = jnp.max(s, axis=-1, keepdims=True)
    p = jnp.exp(s - m)
    l = jnp.sum(p, axis=-1, keepdims=True)
    o = jnp.dot(p.astype(BF16), v_ref[0].astype(BF16), preferred_element_type=F32)
    o_ref[0] = (o / l).astype(o_ref.dtype)


def memory_attention(q_src, q_col0, mem_k, mem_v, k_col0, v_col0, *, tq):
    bsz, t, _ = q_src.shape
    n_mem = mem_k.shape[1]
    qc, kc, vc = q_col0 // HEAD, k_col0 // HEAD, v_col0 // HEAD
    return pl.pallas_call(
        _mem_attn_kernel,
        grid=(bsz, MEM_HEADS, t // tq),
        in_specs=[pl.BlockSpec((1, tq, HEAD), lambda b, h, i: (b, i, qc + h)),
                  pl.BlockSpec((1, n_mem, HEAD), lambda b, h, i: (b, 0, kc + h)),
                  pl.BlockSpec((1, n_mem, HEAD), lambda b, h, i: (b, 0, vc + h))],
        out_specs=pl.BlockSpec((1, tq, HEAD), lambda b, h, i: (b, i, h)),
        out_shape=jax.ShapeDtypeStruct((bsz, t, MEM_WIDTH), BF16),
        compiler_params=_params(3),
        name="memory_attention",
    )(q_src, mem_k, mem_v)


def _band_attn_kernel(q_ref, k_ref, v_ref, e_ref, o_ref, bias_ref, *, front_pad):
    i = pl.program_id(2)

    @pl.when(i == 0)
    def _():
        e = jnp.broadcast_to(e_ref[0], (BAND_TQ, BAND_TABLE))
        bias_ref[...] = pltpu.roll(e, 0, 1, stride=1, stride_axis=0)[:, :BAND_KEYS]

    rows = pl.ds(pl.multiple_of(i * BAND_TQ, BAND_TQ), BAND_KEYS)
    q = (q_ref[0] * (HEAD ** -0.5)).astype(BF16)
    s = _nt_dot(q, k_ref[0, rows, :].astype(BF16)) + bias_ref[...]
    qq = lax.broadcasted_iota(jnp.int32, (BAND_TQ, BAND_KEYS), 0)
    kk = lax.broadcasted_iota(jnp.int32, (BAND_TQ, BAND_KEYS), 1)
    lo = jnp.maximum(front_pad - i * BAND_TQ, jnp.where(qq < CHUNK, 0, CHUNK))
    hi = jnp.where(qq < CHUNK, BAND_KEYS - CHUNK, BAND_KEYS)
    s = jnp.where((kk >= lo) & (kk < hi), s, NEG_INF)
    m = jnp.max(s, axis=-1, keepdims=True)
    p = jnp.exp(s - m)
    l = jnp.sum(p, axis=-1, keepdims=True)
    o = jnp.dot(p.astype(BF16), v_ref[0, rows, :].astype(BF16), preferred_element_type=F32)
    o_ref[0] = (o / l).astype(o_ref.dtype)


def band_attention(q_src, q_col0, k_pad, v_pad, bias_tab, *, front_pad):
    bsz, t, _ = q_src.shape
    s_len = k_pad.shape[1]
    assert t % BAND_TQ == 0 and s_len == t + B_WINDOW
    qc = q_col0 // HEAD
    kern = functools.partial(_band_attn_kernel, front_pad=front_pad)
    return pl.pallas_call(
        kern,
        grid=(bsz, B_HEADS, t // BAND_TQ),
        in_specs=[pl.BlockSpec((1, BAND_TQ, HEAD), lambda b, h, i: (b, i, qc + h)),
                  pl.BlockSpec((1, s_len, HEAD), lambda b, h, i: (b, 0, h)),
                  pl.BlockSpec((1, s_len, HEAD), lambda b, h, i: (b, 0, h)),
                  pl.BlockSpec((1, 1, BAND_TABLE), lambda b, h, i: (h, 0, 0))],
        out_specs=pl.BlockSpec((1, BAND_TQ, HEAD), lambda b, h, i: (b, i, h)),
        out_shape=jax.ShapeDtypeStruct((bsz, t, B_WIDTH), BF16),
        scratch_shapes=[pltpu.VMEM((BAND_TQ, BAND_KEYS), F32)],
        compiler_params=_params(3),
        name="band_attention",
    )(q_src, k_pad, v_pad, bias_tab)


def _band_bias_table(rel_bias):
    c = jnp.arange(BAND_TABLE)
    d = jnp.where(c <= BAND_KEYS, c, c - BAND_TABLE)
    idx = jnp.clip(B_WINDOW - d, -B_REL_CLIP, B_REL_CLIP) + B_REL_CLIP
    return rel_bias[:, idx][:, None, :]


def _stick_attn_kernel(q_ref, k_ref, v_ref, o_ref, acc_ref, run_ref, tri_ref, *, tq, tk, q_off):
    i = pl.program_id(2)
    nblk = (q_off + (i + 1) * tq - 2) // tk + 1

    @pl.when((pl.program_id(0) == 0) & (pl.program_id(1) == 0) & (i == 0))
    def _():
        r = lax.broadcasted_iota(jnp.int32, (tk, tk), 0)
        c = lax.broadcasted_iota(jnp.int32, (tk, tk), 1)
        tri_ref[...] = jnp.where(r > c, 1.0, 0.0).astype(BF16)

    q = (q_ref[0] * (HEAD ** -0.5)).astype(BF16)
    acc_ref[...] = jnp.zeros_like(acc_ref)
    run_ref[...] = jnp.zeros_like(run_ref)

    def block(j, masked):
        rows = pl.ds(pl.multiple_of(j * tk, tk), tk)
        z = _nt_dot(q, k_ref[0, rows, :].astype(BF16))
        log_1m_beta = -(jnp.maximum(z, 0.0) + jnp.log1p(jnp.exp(-jnp.abs(z))))
        log_beta = z + log_1m_beta
        if masked:
            qpos = q_off + i * tq + lax.broadcasted_iota(jnp.int32, (tq, tk), 0)
            kpos = j * tk + lax.broadcasted_iota(jnp.int32, (tq, tk), 1)
            allowed = kpos < qpos
            log_1m_beta = jnp.where(allowed, log_1m_beta, 0.0)
        hi = log_1m_beta.astype(BF16)
        lo = (log_1m_beta - hi.astype(F32)).astype(BF16)
        tri = tri_ref[...]
        tail = (jnp.dot(hi, tri, preferred_element_type=F32)
                + jnp.dot(lo, tri, preferred_element_type=F32) + run_ref[...])
        a = jnp.exp(log_beta + tail)
        if masked:
            a = jnp.where(allowed, a, 0.0)
        acc_ref[...] += jnp.dot(a.astype(BF16), v_ref[0, rows, :].astype(BF16), preferred_element_type=F32)
        run_ref[...] += jnp.sum(log_1m_beta, axis=-1, keepdims=True)

    block(nblk - 1, True)

    def body(step, carry):
        block(nblk - 2 - step, False)
        return carry

    lax.fori_loop(0, nblk - 1, body, 0)
    o_ref[0] = acc_ref[...].astype(o_ref.dtype)


def stick_attention(q_src, q_col0, k_rows, v_rows, *, tq, tk, q_off):
    bsz, t, _ = q_src.shape
    s_len = k_rows.shape[1]
    assert s_len % tk == 0 and t % tq == 0
    qc = q_col0 // HEAD
    kern = functools.partial(_stick_attn_kernel, tq=tq, tk=tk, q_off=q_off)
    return pl.pallas_call(
        kern,
        grid=(bsz, D_HEADS, t // tq),
        in_specs=[pl.BlockSpec((1, tq, HEAD), lambda b, h, i: (b, i, qc + h)),
                  pl.BlockSpec((1, s_len, HEAD), lambda b, h, i: (b, 0, h)),
                  pl.BlockSpec((1, s_len, HEAD), lambda b, h, i: (b, 0, h))],
        out_specs=pl.BlockSpec((1, tq, HEAD), lambda b, h, i: (b, i, h)),
        out_shape=jax.ShapeDtypeStruct((bsz, t, D_WIDTH), BF16),
        scratch_shapes=[pltpu.VMEM((tq, HEAD), F32), pltpu.VMEM((tq, 1), F32), pltpu.VMEM((tk, tk), BF16)],
        compiler_params=_params(3),
        name="stick_attention",
    )(q_src, k_rows, v_rows)


def _pad_rows(a, front, back):
    return jnp.pad(a, ((0, 0), (front, back), (0, 0)))


def _round_up(n, mult):
    return (n + mult - 1) // mult * mult


def _mixer_diff(z, pos, past, params, lam_init):
    lq1, lk1, lq2, lk2, norm_g = params
    bsz, t, _ = z.shape
    tables = _rope_tables(pos, A_ROT_DIM // 2, ROPE_THETA)
    k_rows = rope_heads(z, A_QK_WIDTH, 2 * A_HEADS, tables, A_ROT_DIM // 2, tt=min(t, 512))
    v_rows = z[:, :, 2 * A_QK_WIDTH:2 * A_QK_WIDTH + MIX_WIDTH]
    lam_vecs = jnp.stack([lq1, lq2, lk1, lk2]).astype(F32)
    g = norm_g.reshape(1, 2 * HEAD).astype(F32)
    if past is None:
        k_all, v_all, q_off, tq, tk = k_rows, v_rows, 0, 256, 256
    else:
        p_len = past[0].shape[1]
        tq, tk, q_off = t, 256, p_len
        s_pad = _round_up(p_len + t, tk)
        k_all = _pad_rows(jnp.concatenate([past[0].reshape(bsz, p_len, MIX_WIDTH), k_rows], axis=1),
                          0, s_pad - p_len - t)
        v_all = _pad_rows(jnp.concatenate([past[1].reshape(bsz, p_len, MIX_WIDTH), v_rows], axis=1),
                          0, s_pad - p_len - t)
    o = diff_attention(z, 0, k_all, v_all, tables, lam_vecs, g, tq=tq, tk=tk, q_off=q_off, lam_init=lam_init)
    shape = (bsz, t, A_HEADS, 2 * HEAD)
    return o, (k_rows.reshape(shape), v_rows.reshape(shape))


def _mixer_band(z, pos, past, params):
    (rel_bias,) = params
    bsz, t, _ = z.shape
    k = z[:, :, B_WIDTH:2 * B_WIDTH]
    v = z[:, :, 2 * B_WIDTH:3 * B_WIDTH]
    bias_tab = _band_bias_table(rel_bias.astype(F32))
    shape = (bsz, -1, B_HEADS, HEAD)
    if past is None:
        o = band_attention(z, 0, _pad_rows(k, B_WINDOW, 0), _pad_rows(v, B_WINDOW, 0), bias_tab,
                           front_pad=B_WINDOW)
        keep = min(B_WINDOW, t)
        state = (k[:, t - keep:].reshape(shape), v[:, t - keep:].reshape(shape))
    else:
        buf_len = past[0].shape[1]
        assert t == CHUNK and buf_len == B_WINDOW
        k_all = jnp.concatenate([past[0].reshape(bsz, buf_len, B_WIDTH), k], axis=1)
        v_all = jnp.concatenate([past[1].reshape(bsz, buf_len, B_WIDTH), v], axis=1)
        q_pad = _pad_rows(z[:, :, :B_WIDTH], BAND_TQ - t, 0)
        o = band_attention(q_pad, 0, _pad_rows(k_all, CHUNK, 0), _pad_rows(v_all, CHUNK, 0), bias_tab,
                           front_pad=CHUNK)[:, BAND_TQ - t:]
        state = (k_all[:, t:].reshape(shape), v_all[:, t:].reshape(shape))
    return o, state


def _mixer_mla(z, pos, past, params, w_uq, w_ukv):
    q_norm_g, kv_norm_g = params
    bsz, t, _ = z.shape
    half = C_ROPE_DIM // 2
    tables = _rope_tables(pos, half, C_ROPE_THETA)
    cq, latent, kr = mla_prep(z.reshape(bsz * t, C_IN_PAD), q_norm_g.reshape(1, -1).astype(F32),
                              kv_norm_g.reshape(1, -1).astype(F32), tables, seq_len=t, tt=min(t, 512))
    q = matmul(cq, w_uq, tm=min(bsz * t, 1024), tn=1024).reshape(bsz, t, -1)
    latent = latent.reshape(bsz, t, C_KV_RANK)
    kr = kr.reshape(bsz, t, HEAD)
    if past is None:
        lat_all, kr_all, q_off, tq, tk = latent, kr, 0, 256, 256
    else:
        p_len = past[0].shape[1]
        tq, tk, q_off = t, 256, p_len
        back = _round_up(p_len + t, tk) - p_len - t
        lat_all = _pad_rows(jnp.concatenate([past[0], latent], axis=1), 0, back)
        kr_past = jnp.pad(past[1], ((0, 0), (0, 0), (0, HEAD - C_ROPE_DIM)))
        kr_all = _pad_rows(jnp.concatenate([kr_past, kr], axis=1), 0, back)
    s_len = lat_all.shape[1]
    kv = matmul(lat_all.reshape(bsz * s_len, C_KV_RANK), w_ukv, tm=_pick(bsz * s_len, (1024, 512, 256)),
                tn=1024).reshape(bsz, s_len, -1)
    o = mla_attention(q, kv, kr_all, tables, tq=tq, tk=tk, q_off=q_off)
    return o, (latent, kr[:, :, :C_ROPE_DIM])


def _mixer_stick(z, pos, past):
    bsz, t, _ = z.shape
    k = z[:, :, D_WIDTH:2 * D_WIDTH]
    v = z[:, :, 2 * D_WIDTH:3 * D_WIDTH]
    if past is None:
        k_all, v_all, q_off, tq, tk = k, v, 0, 256, 256
    else:
        p_len = past[0].shape[1]
        tq, tk, q_off = t, 256, p_len
        back = _round_up(p_len + t, tk) - p_len - t
        k_all = _pad_rows(jnp.concatenate([past[0].reshape(bsz, p_len, D_WIDTH), k], axis=1), 0, back)
        v_all = _pad_rows(jnp.concatenate([past[1].reshape(bsz, p_len, D_WIDTH), v], axis=1), 0, back)
    o = stick_attention(z, 0, k_all, v_all, tq=tq, tk=tk, q_off=q_off)
    shape = (bsz, t, D_HEADS, HEAD)
    return o, (k.reshape(shape), v.reshape(shape))


def _trunk_layer(layer, x, pos, past, mem_kv, conv_state, mix_params, w):
    bsz, t, _ = x.shape
    m = bsz * t
    mixer = layer % N_MIXERS
    x2d = x.reshape(m, D_MODEL)
    tm = min(m, 1024)
    z = matmul(x2d, w["w_in"], tm=tm, tn=_pick(w["w_in"].shape[1], (1024, 640))).reshape(bsz, t, -1)
    if mixer == 0:
        o_mix, state = _mixer_diff(z, pos, past, mix_params, 0.8 - 0.6 * math.exp(-0.3 * layer))
    elif mixer == 1:
        o_mix, state = _mixer_band(z, pos, past, mix_params)
    elif mixer == 2:
        o_mix, state = _mixer_mla(z, pos, past, mix_params, w["w_uq"], w["w_ukv"])
    else:
        o_mix, state = _mixer_stick(z, pos, past)
    q_mem_col0 = C_Q_RANK + C_KV_RANK if mixer == 2 else 3 * MIX_WIDTH
    o_mem = memory_attention(z, q_mem_col0, mem_kv[0], mem_kv[1], mem_kv[2], mem_kv[3], tq=min(t, 512))
    tm_ln = min(m, 256)
    x1 = out_proj_ln(o_mix.reshape(m, MIX_WIDTH), o_mem.reshape(m, MEM_WIDTH), w["w_o"], x2d,
                     w["ln1_g"], w["ln1_b"], tm=tm_ln)
    g = ffn_up(x1, w["w_up"], w["conv_w"], w["conv_b"], conv_state, seq_len=t, tm=min(t, 512), tn=512)
    x2 = down_proj_ln(g, w["w_down"], x1, w["ln2_g"], w["ln2_b"], tm=min(m, 512), tk=1408)
    return x2.reshape(bsz, t, D_MODEL), state, x1.reshape(bsz, t, D_MODEL)


def _reorder_w_in_c(w):
    a, b = C_Q_RANK + C_KV_RANK, C_Q_RANK + C_KV_RANK + C_ROPE_DIM
    pad = jnp.zeros((w.shape[0], C_IN_PAD - w.shape[1]), w.dtype)
    return jnp.concatenate([w[:, :a], w[:, b:], w[:, a:b], pad], axis=1)


def _reorder_w_uq(w):
    w = w.reshape(C_Q_RANK, C_HEADS, C_NOPE_DIM + C_ROPE_DIM)
    nope = w[:, :, :C_NOPE_DIM].reshape(C_Q_RANK, C_HEADS * HEAD)
    rope = jnp.pad(w[:, :, C_NOPE_DIM:], ((0, 0), (0, 0), (0, HEAD - C_ROPE_DIM)))
    return jnp.concatenate([nope, rope.reshape(C_Q_RANK, C_HEADS * HEAD)], axis=1)


def kernel(x_prompt, x_sample, mem_prompt, cache_a_k, cache_a_v, cache_b_k, cache_b_v, cache_c_latent, cache_c_krope, cache_d_k, cache_d_v, cache_mem_k, cache_mem_v, state_ffn_conv, w_in_a, w_in_b, w_in_c, w_in_d, diff_lambda_q1, diff_lambda_k1, diff_lambda_q2, diff_lambda_k2, diff_norm_g, band_rel_bias, mla_q_norm_g, mla_kv_norm_g, mla_w_uq, mla_w_ukv, w_mem_kv, w_o, ln1_g, ln1_b, w_up, conv_ffn_w, conv_ffn_b, w_down, ln2_g, ln2_b):
    n_p, t_p, _ = x_prompt.shape
    n_s, t_s, _ = x_sample.shape
    past_len = cache_d_k.shape[2]
    pos_p = jnp.arange(t_p)
    pos_s = past_len + jnp.arange(t_s)
    w_in_by_type = (w_in_a, w_in_b, w_in_c, w_in_d)
    caches_by_type = ((cache_a_k, cache_a_v), (cache_b_k, cache_b_v),
                      (cache_c_latent, cache_c_krope), (cache_d_k, cache_d_v))
    params_by_type = ((diff_lambda_q1, diff_lambda_k1, diff_lambda_q2, diff_lambda_k2, diff_norm_g),
                      (band_rel_bias,), (mla_q_norm_g, mla_kv_norm_g), ())
    states_p = [([], []) for _ in range(N_MIXERS)]
    states_s = [([], []) for _ in range(N_MIXERS)]
    mem_k_p, mem_v_p, conv_p, conv_s = [], [], [], []
    n_mem = mem_prompt.shape[1]
    mem2d = mem_prompt.reshape(n_p * n_mem, D_MODEL)
    zero_state = jnp.zeros((n_p, 2, 2 * D_FF), F32)
    x_p, x_s = x_prompt, x_sample
    for i in range(DEPTH):
        mixer, j = i % N_MIXERS, i // N_MIXERS
        w_in = w_in_by_type[mixer][j]
        w = {
            "w_in": (_reorder_w_in_c(w_in) if mixer == 2 else w_in).astype(BF16),
            "w_o": w_o[i].astype(BF16),
            "ln1_g": ln1_g[i].reshape(1, -1), "ln1_b": ln1_b[i].reshape(1, -1),
            "w_up": w_up[i].astype(BF16),
            "conv_w": conv_ffn_w[i], "conv_b": conv_ffn_b[i].reshape(1, -1),
            "w_down": w_down[i].astype(BF16),
            "ln2_g": ln2_g[i].reshape(1, -1), "ln2_b": ln2_b[i].reshape(1, -1),
        }
        if mixer == 2:
            w["w_uq"] = _reorder_w_uq(mla_w_uq[j]).astype(BF16)
            w["w_ukv"] = mla_w_ukv[j].astype(BF16)
        mix_params = tuple(p[j] for p in params_by_type[mixer])
        kv_mem = matmul(mem2d, w_mem_kv[i].astype(BF16), tm=n_p * n_mem, tn=512).reshape(n_p, n_mem, 2 * MEM_WIDTH)
        x_p, st_p, x1_p = _trunk_layer(i, x_p, pos_p, None, (kv_mem, kv_mem, 0, MEM_WIDTH), zero_state,
                                       mix_params, w)
        past = (caches_by_type[mixer][0][j], caches_by_type[mixer][1][j])
        mem_s = (cache_mem_k[i].reshape(n_s, n_mem, MEM_WIDTH), cache_mem_v[i].reshape(n_s, n_mem, MEM_WIDTH), 0, 0)
        x_s, st_s, x1_s = _trunk_layer(i, x_s, pos_s, past, mem_s, state_ffn_conv[i], mix_params, w)
        tails = jnp.concatenate([x1_p[:, t_p - 2:].reshape(2 * n_p, D_MODEL),
                                 x1_s[:, t_s - 2:].reshape(2 * n_s, D_MODEL)], axis=0)
        n_tail = tails.shape[0]
        tails = jnp.pad(tails, ((0, _round_up(n_tail, 16) - n_tail), (0, 0)))
        u_tail = matmul(tails, w["w_up"], tm=tails.shape[0], tn=1024)
        conv_p.append(u_tail[:2 * n_p].reshape(n_p, 2, 2 * D_FF))
        conv_s.append(u_tail[2 * n_p:n_tail].reshape(n_s, 2, 2 * D_FF))
        for a in range(2):
            states_p[mixer][a].append(st_p[a])
            states_s[mixer][a].append(st_s[a])
        mem_k_p.append(kv_mem[:, :, :MEM_WIDTH].reshape(n_p, n_mem, MEM_HEADS, HEAD))
        mem_v_p.append(kv_mem[:, :, MEM_WIDTH:].reshape(n_p, n_mem, MEM_HEADS, HEAD))
    outs = [x_p, x_s]
    for st in states_p:
        outs += [jnp.stack(st[0]), jnp.stack(st[1])]
    outs += [jnp.stack(mem_k_p), jnp.stack(mem_v_p), jnp.stack(conv_p)]
    for st in states_s:
        outs += [jnp.stack(st[0]), jnp.stack(st[1])]
    outs.append(jnp.stack(conv_s))
    return tuple(outs)
```

```python
import functools
import math

import jax
import jax.numpy as jnp
from jax import lax
from jax.experimental import pallas as pl
from jax.experimental.pallas import tpu as pltpu

F32 = jnp.float32
BF16 = jnp.bfloat16

D_MODEL = 2048
DEPTH = 4
CHUNK = 64
CHUNK_SHIFT = 6
N_MIXERS = 4
MIX_WIDTH = 1536
MEM_HEADS = 4
MEM_WIDTH = 512
HEAD = 128
ROPE_THETA = 500000.0

A_HEADS = 6
A_QK_WIDTH = 1536
A_ROT_DIM = 32

B_HEADS = 12
B_WIDTH = 1536
B_WINDOW = 512
B_REL_CLIP = 128
BAND_TQ = 128
BAND_KEYS = B_WINDOW + BAND_TQ
BAND_TABLE = 768

C_HEADS = 12
C_Q_RANK = 768
C_KV_RANK = 512
C_NOPE_DIM = 128
C_ROPE_DIM = 64
C_ROPE_THETA = 10000.0
C_KR_COL = C_Q_RANK + C_KV_RANK
C_MEM_COL = 1536
C_IN_PAD = C_MEM_COL + MEM_WIDTH

D_HEADS = 12
D_WIDTH = 1536

D_FF = 5632
FFN_HALO = 16

ATT_TQ = 256
ATT_BLK = 256
ATT_WIDE = 512

DEEPNORM_ALPHA = (2 * DEPTH) ** 0.25
NORM_EPS = 1e-5
NEG_INF = -1e30
LOG2E = 1.0 / math.log(2.0)

VMEM_LIMIT = 56 * 1024 * 1024


def _params(n_axes):
    return pltpu.CompilerParams(dimension_semantics=("arbitrary",) * n_axes,
                                vmem_limit_bytes=VMEM_LIMIT)


def _pick(n, cands):
    for c in cands:
        if n % c == 0:
            return c
    return n


def _mm_kernel(a_ref, b_ref, *rest):
    *o_refs, abf_ref = rest

    @pl.when(pl.program_id(1) == 0)
    def _():
        abf_ref[...] = a_ref[...].astype(BF16)

    r = jnp.dot(abf_ref[...], b_ref[...], preferred_element_type=F32)
    for o_ref in o_refs:
        o_ref[...] = r.astype(o_ref.dtype)


def matmul(a, b, *, tm, tn, out_dtypes=(F32,)):
    m, k = a.shape
    n = b.shape[1]
    assert m % tm == 0 and n % tn == 0, (a.shape, b.shape, tm, tn)
    outs = pl.pallas_call(
        _mm_kernel,
        grid=(m // tm, n // tn),
        in_specs=[pl.BlockSpec((tm, k), lambda i, j: (i, 0)),
                  pl.BlockSpec((k, tn), lambda i, j: (0, j))],
        out_specs=[pl.BlockSpec((tm, tn), lambda i, j: (i, j)) for _ in out_dtypes],
        out_shape=[jax.ShapeDtypeStruct((m, n), dt) for dt in out_dtypes],
        scratch_shapes=[pltpu.VMEM((tm, k), BF16)],
        compiler_params=_params(2),
        name="matmul",
    )(a, b)
    return outs[0] if len(out_dtypes) == 1 else outs


def _layer_norm_rows(y, g, b):
    mu = jnp.mean(y, axis=-1, keepdims=True)
    d = y - mu
    var = jnp.mean(d * d, axis=-1, keepdims=True)
    return d * lax.rsqrt(var + NORM_EPS) * g + b


def _proj_ln_kernel(*refs, n_parts):
    a_refs = refs[:n_parts]
    w_ref, x_ref, g_ref, b_ref, o_ref = refs[n_parts:]
    f, k0 = None, 0
    for a_ref in a_refs:
        k = a_ref.shape[1]
        part = jnp.dot(a_ref[...], w_ref[k0:k0 + k, :], preferred_element_type=F32)
        f = part if f is None else f + part
        k0 += k
    y = DEEPNORM_ALPHA * x_ref[...] + f
    o_ref[...] = _layer_norm_rows(y, g_ref[...], b_ref[...])


def proj_ln(a_parts, w, x, g, b, *, tm, name):
    m = x.shape[0]
    k_total, n = w.shape
    assert sum(a.shape[1] for a in a_parts) == k_total and m % tm == 0
    row = lambda i: (i, 0)
    fixed = lambda i: (0, 0)
    return pl.pallas_call(
        functools.partial(_proj_ln_kernel, n_parts=len(a_parts)),
        grid=(m // tm,),
        in_specs=([pl.BlockSpec((tm, a.shape[1]), row) for a in a_parts]
                  + [pl.BlockSpec((k_total, n), fixed, pipeline_mode=pl.Buffered(1)),
                     pl.BlockSpec((tm, n), row), pl.BlockSpec((1, n), fixed), pl.BlockSpec((1, n), fixed)]),
        out_specs=pl.BlockSpec((tm, n), row),
        out_shape=jax.ShapeDtypeStruct((m, n), F32),
        compiler_params=_params(1),
        name=name,
    )(*a_parts, w, x, g, b)


def _silu(x):
    return x * (1.0 / (1.0 + jnp.exp(-x)))


def _ffn_up_kernel(x_ref, xh_ref, wg_ref, wv_ref, cwg_ref, cwv_ref, cbg_ref, cbv_ref,
                   sg_ref, sv_ref, o_ref, xcat_ref, *, tiles_per_seq, seq_starts):
    i = pl.program_id(0)
    j = pl.program_id(1)

    @pl.when(j == 0)
    def _():
        xcat_ref[:FFN_HALO, :] = xh_ref[...].astype(BF16)
        xcat_ref[FFN_HALO:, :] = x_ref[...].astype(BF16)

    def conv(u, cw_ref, cb_ref):
        cw = cw_ref[...]
        p1 = pltpu.roll(u, 1, 0)[FFN_HALO:]
        p2 = pltpu.roll(u, 2, 0)[FFN_HALO:]
        return cb_ref[...] + cw[0:1] * p2 + cw[1:2] * p1 + cw[2:3] * u[FFN_HALO:]

    xe = xcat_ref[...]
    ug = jnp.dot(xe, wg_ref[...], preferred_element_type=F32)
    gate = _silu(conv(ug, cwg_ref, cbg_ref))
    uv = jnp.dot(xe, wv_ref[...], preferred_element_type=F32)
    o_ref[...] = (gate * conv(uv, cwv_ref, cbv_ref)).astype(o_ref.dtype)

    def fix_sequence_start(seq, row0):
        rid = lax.broadcasted_iota(jnp.int32, (FFN_HALO, o_ref.shape[1]), 0)

        def conv_head(u, cw_ref, cb_ref, st_ref):
            cw = cw_ref[...]
            cur = u[FFN_HALO + row0:2 * FFN_HALO + row0]
            st = st_ref[seq]
            s0, s1 = st[0:1], st[1:2]
            p1 = jnp.where(rid == 0, s1, pltpu.roll(cur, 1, 0))
            p2 = jnp.where(rid == 0, s0, jnp.where(rid == 1, s1, pltpu.roll(cur, 2, 0)))
            return cb_ref[...] + cw[0:1] * p2 + cw[1:2] * p1 + cw[2:3] * cur

        hg = conv_head(ug, cwg_ref, cbg_ref, sg_ref)
        hv = conv_head(uv, cwv_ref, cbv_ref, sv_ref)
        o_ref[row0:row0 + FFN_HALO, :] = (_silu(hg) * hv).astype(o_ref.dtype)

    if tiles_per_seq > 1:
        pl.when(i % tiles_per_seq == 0)(lambda: fix_sequence_start(0, 0))
    else:
        for seq, row0 in enumerate(seq_starts):
            fix_sequence_start(seq, row0)


def ffn_up(x, w_up, conv_w, conv_b, state, *, seq_len, tm, tn):
    m, k = x.shape
    assert (seq_len % tm == 0 or tm % seq_len == 0) and seq_len % FFN_HALO == 0 and D_FF % tn == 0
    tiles_per_seq = max(seq_len // tm, 1)
    seqs_per_tile = max(tm // seq_len, 1)
    nf = D_FF // tn
    halo_blocks = tm // FFN_HALO
    kern = functools.partial(_ffn_up_kernel, tiles_per_seq=tiles_per_seq,
                             seq_starts=tuple(q * seq_len for q in range(seqs_per_tile)))
    seq_block = lambda i: i * seqs_per_tile // tiles_per_seq // seqs_per_tile
    return pl.pallas_call(
        kern,
        grid=(m // tm, nf),
        in_specs=[
            pl.BlockSpec((tm, k), lambda i, j: (i, 0)),
            pl.BlockSpec((FFN_HALO, k), lambda i, j: (jnp.maximum(i * halo_blocks - 1, 0), 0)),
            pl.BlockSpec((k, tn), lambda i, j: (0, j)),
            pl.BlockSpec((k, tn), lambda i, j: (0, j + nf)),
            pl.BlockSpec((3, tn), lambda i, j: (0, j)),
            pl.BlockSpec((3, tn), lambda i, j: (0, j + nf)),
            pl.BlockSpec((1, tn), lambda i, j: (0, j)),
            pl.BlockSpec((1, tn), lambda i, j: (0, j + nf)),
            pl.BlockSpec((seqs_per_tile, 2, tn), lambda i, j: (seq_block(i), 0, j)),
            pl.BlockSpec((seqs_per_tile, 2, tn), lambda i, j: (seq_block(i), 0, j + nf)),
        ],
        out_specs=pl.BlockSpec((tm, tn), lambda i, j: (i, j)),
        out_shape=jax.ShapeDtypeStruct((m, D_FF), BF16),
        scratch_shapes=[pltpu.VMEM((tm + FFN_HALO, k), BF16)],
        compiler_params=_params(2),
        name="ffn_up",
    )(x, x, w_up, w_up, conv_w, conv_w, conv_b, conv_b, state, state)


def _rope_tables(pos, half, theta):
    inv_freq = theta ** (-jnp.arange(half, dtype=F32) / half)
    ang = pos.astype(F32)[:, None] * inv_freq[None, :]
    cos, sin = jnp.cos(ang), jnp.sin(ang)
    t = pos.shape[0]
    rest = HEAD - 2 * half
    cos_t = jnp.concatenate([cos, cos, jnp.ones((t, rest), F32)], axis=1)
    sin_up = jnp.concatenate([jnp.zeros((t, half), F32), sin, jnp.zeros((t, rest), F32)], axis=1)
    sin_dn = jnp.concatenate([-sin, jnp.zeros((t, half + rest), F32)], axis=1)
    return cos_t, sin_up, sin_dn


def _rope(x, cos_t, sin_up, sin_dn, half):
    return x * cos_t + pltpu.roll(x, half, 1) * sin_up + pltpu.roll(x, HEAD - half, 1) * sin_dn


def _rope_heads_kernel(x_ref, c_ref, su_ref, sd_ref, o_ref, ob_ref, *, half, n_heads):
    tabs = (c_ref[...], su_ref[...], sd_ref[...])
    for h in range(n_heads):
        cols = slice(h * HEAD, (h + 1) * HEAD)
        r = _rope(x_ref[0, :, cols], *tabs, half)
        o_ref[0, :, cols] = r
        ob_ref[0, :, cols] = r.astype(BF16)


def rope_heads(z, col0, n_heads, tables, half, *, tt):
    bsz, t, _ = z.shape
    w = n_heads * HEAD
    assert col0 % w == 0
    c0 = col0 // w
    tab = pl.BlockSpec((tt, HEAD), lambda b, i: (i, 0))
    out = pl.BlockSpec((1, tt, w), lambda b, i: (b, i, 0))
    return pl.pallas_call(
        functools.partial(_rope_heads_kernel, half=half, n_heads=n_heads),
        grid=(bsz, t // tt),
        in_specs=[pl.BlockSpec((1, tt, w), lambda b, i: (b, i, c0)), tab, tab, tab],
        out_specs=[out, out],
        out_shape=[jax.ShapeDtypeStruct((bsz, t, w), F32), jax.ShapeDtypeStruct((bsz, t, w), BF16)],
        compiler_params=_params(2),
        name="rope_heads",
    )(z, *tables)


def _rms_rows(x, g):
    return x * lax.rsqrt(jnp.mean(x * x, axis=-1, keepdims=True) + NORM_EPS) * g


def _mla_prep_kernel(z_ref, gq_ref, gkv_ref, c_ref, su_ref, sd_ref, cq_ref, lat_ref, kr_ref):
    z = z_ref[...]
    cq_ref[...] = _rms_rows(z[:, :C_Q_RANK], gq_ref[...]).astype(cq_ref.dtype)
    lat_ref[...] = _rms_rows(z[:, C_Q_RANK:C_KR_COL], gkv_ref[...])
    kr = z[:, C_KR_COL:C_KR_COL + HEAD]
    kr_ref[...] = _rope(kr, c_ref[...], su_ref[...], sd_ref[...], C_ROPE_DIM // 2)


def mla_prep(z2d, gq, gkv, tables, *, seq_len, tt):
    m = z2d.shape[0]
    nt = seq_len // tt
    row = lambda i: (i, 0)
    fixed = lambda i: (0, 0)
    tab = pl.BlockSpec((tt, HEAD), lambda i: (i % nt, 0))
    return pl.pallas_call(
        _mla_prep_kernel,
        grid=(m // tt,),
        in_specs=[pl.BlockSpec((tt, C_MEM_COL), row), pl.BlockSpec((1, C_Q_RANK), fixed),
                  pl.BlockSpec((1, C_KV_RANK), fixed), tab, tab, tab],
        out_specs=[pl.BlockSpec((tt, C_Q_RANK), row), pl.BlockSpec((tt, C_KV_RANK), row),
                   pl.BlockSpec((tt, HEAD), row)],
        out_shape=[jax.ShapeDtypeStruct((m, C_Q_RANK), BF16),
                   jax.ShapeDtypeStruct((m, C_KV_RANK), F32),
                   jax.ShapeDtypeStruct((m, HEAD), F32)],
        compiler_params=_params(1),
        name="mla_prep",
    )(z2d, gq, gkv, *tables)


def _nt_dot(a, b):
    return lax.dot_general(a, b, (((1,), (1,)), ((), ())), preferred_element_type=F32)


def _lanes(x, n):
    return x if n == HEAD else jnp.concatenate([x] * (n // HEAD), axis=1)


def _chunk_mask(tq, tk, q_start, k_start):
    qpos = q_start + lax.broadcasted_iota(jnp.int32, (tq, tk), 0)
    kpos = k_start + lax.broadcasted_iota(jnp.int32, (tq, tk), 1)
    return jnp.right_shift(kpos, CHUNK_SHIFT) <= jnp.right_shift(qpos, CHUNK_SHIFT)


def _softmax_steps(scores, values, stats):
    probs, alphas = [], []
    for s, (m_ref, l_ref, _) in zip(scores, stats):
        tk = s.shape[1]
        m_old = m_ref[...]
        m_new = jnp.maximum(m_old, jnp.max(s, axis=-1, keepdims=True))
        alpha = jnp.exp2(m_old - m_new)
        p = jnp.exp2(s - _lanes(m_new, tk))
        psum = p[:, :HEAD]
        for c in range(1, tk // HEAD):
            psum = psum + p[:, c * HEAD:(c + 1) * HEAD]
        l_ref[...] = alpha * l_ref[...] + psum
        m_ref[...] = m_new
        probs.append(p.astype(BF16))
        alphas.append(alpha)
    for p, v, alpha, (_, _, acc_ref) in zip(probs, values, alphas, stats):
        acc_ref[...] = (_lanes(alpha, acc_ref.shape[-1]) * acc_ref[...]
                        + jnp.dot(p, v, preferred_element_type=F32))


def _softmax_scratch(n_chains, tq, dv):
    return [pltpu.VMEM((tq, HEAD), F32), pltpu.VMEM((tq, HEAD), F32), pltpu.VMEM((tq, dv), F32)] * n_chains


def _softmax_stats(scratch_refs):
    stats = [tuple(scratch_refs[3 * c:3 * c + 3]) for c in range(len(scratch_refs) // 3)]
    for m_ref, l_ref, acc_ref in stats:
        m_ref[...] = jnp.full_like(m_ref, NEG_INF)
        l_ref[...] = jnp.zeros_like(l_ref)
        acc_ref[...] = jnp.zeros_like(acc_ref)
    return stats


def _softmax_result(stat):
    _, l_ref, acc_ref = stat
    return acc_ref[...] / jnp.sum(l_ref[...], axis=-1, keepdims=True)


def _causal_key_blocks(block, q_start):
    n_wide = q_start // ATT_WIDE

    def body(j, carry):
        block(pl.multiple_of(j * ATT_WIDE, ATT_WIDE), ATT_WIDE, False)
        return carry

    lax.fori_loop(0, n_wide, body, 0)
    rest = n_wide * ATT_WIDE

    @pl.when(q_start - rest >= ATT_BLK)
    def _():
        block(pl.multiple_of(rest, ATT_BLK), ATT_BLK, False)

    block(pl.multiple_of(q_start, ATT_BLK), ATT_BLK, True)


def _check_tiling(t, tq, q_off, s_len):
    assert t % tq == 0 and q_off % ATT_BLK == 0 and s_len % ATT_BLK == 0
    assert tq == ATT_BLK or (t == tq and tq <= ATT_BLK), "own chunks must sit in one ATT_BLK key block"
    assert q_off + t <= s_len


def _diff_attn_kernel(q_ref, k_ref, v_ref, c_ref, su_ref, sd_ref, lam_ref, g_ref, o_ref,
                      *scratch_refs, tq, q_off, lam_init, heads):
    i = pl.program_id(2)
    q_start = q_off + i * tq
    scale = HEAD ** -0.5 * LOG2E
    tabs = (c_ref[...], su_ref[...], sd_ref[...])
    n_slots = 2 * heads
    qs = [(_rope(q_ref[0, :, sl * HEAD:(sl + 1) * HEAD].astype(F32), *tabs, A_ROT_DIM // 2) * scale).astype(BF16)
          for sl in range(n_slots)]
    stats = _softmax_stats(scratch_refs)

    def block(start, width, masked):
        rows = pl.ds(start, width)
        scores = [_nt_dot(qs[sl], k_ref[0, rows, sl * HEAD:(sl + 1) * HEAD]) for sl in range(n_slots)]
        if masked:
            mask = _chunk_mask(tq, width, q_start, start)
            scores = [jnp.where(mask, s, NEG_INF) for s in scores]
        values = [v_ref[0, rows, (sl // 2) * 2 * HEAD:(sl // 2 + 1) * 2 * HEAD] for sl in range(n_slots)]
        _softmax_steps(scores, values, stats)

    _causal_key_blocks(block, q_start)

    lam_v = lam_ref[...]
    dots = jnp.sum(lam_v[0:2] * lam_v[2:4], axis=-1, keepdims=True)
    lam = jnp.exp(dots[0:1]) - jnp.exp(dots[1:2]) + lam_init
    for g in range(heads):
        o = _softmax_result(stats[2 * g]) - lam * _softmax_result(stats[2 * g + 1])
        o_ref[0, :, g * 2 * HEAD:(g + 1) * 2 * HEAD] = (
            _rms_rows(o, g_ref[...]) * (1.0 - lam_init)).astype(o_ref.dtype)


def diff_attention(q_src, k_rows, v_src, v_col0, q_tables, lam_vecs, norm_g, *, tq, q_off, lam_init, heads):
    bsz, t, _ = q_src.shape
    s_len = k_rows.shape[1]
    _check_tiling(t, tq, q_off, s_len)
    w = heads * 2 * HEAD
    assert A_HEADS % heads == 0 and v_col0 % w == 0
    vc = v_col0 // w
    tab = pl.BlockSpec((tq, HEAD), lambda b, h, i: (i, 0))
    kern = functools.partial(_diff_attn_kernel, tq=tq, q_off=q_off, lam_init=lam_init, heads=heads)
    return pl.pallas_call(
        kern,
        grid=(bsz, A_HEADS // heads, t // tq),
        in_specs=[pl.BlockSpec((1, tq, w), lambda b, h, i: (b, i, h)),
                  pl.BlockSpec((1, s_len, w), lambda b, h, i: (b, 0, h)),
                  pl.BlockSpec((1, s_len, w), lambda b, h, i: (b, 0, vc + h)),
                  tab, tab, tab,
                  pl.BlockSpec((4, HEAD), lambda b, h, i: (0, 0)),
                  pl.BlockSpec((1, 2 * HEAD), lambda b, h, i: (0, 0))],
        out_specs=pl.BlockSpec((1, tq, w), lambda b, h, i: (b, i, h)),
        out_shape=jax.ShapeDtypeStruct((bsz, t, A_HEADS * 2 * HEAD), BF16),
        scratch_shapes=_softmax_scratch(2 * heads, tq, 2 * HEAD),
        compiler_params=_params(3),
        name="diff_attention",
    )(q_src, k_rows, v_src, *q_tables, lam_vecs, norm_g)


def _mla_attn_kernel(qn_ref, qr_ref, kv_ref, kr_ref, c_ref, su_ref, sd_ref, o_ref,
                     *scratch_refs, tq, q_off, heads):
    i = pl.program_id(2)
    q_start = q_off + i * tq
    scale = (C_NOPE_DIM + C_ROPE_DIM) ** -0.5 * LOG2E
    tabs = (c_ref[...], su_ref[...], sd_ref[...])
    qn = [(qn_ref[0, :, g * HEAD:(g + 1) * HEAD].astype(F32) * scale).astype(BF16) for g in range(heads)]
    qr = [(_rope(qr_ref[0, :, g * HEAD:(g + 1) * HEAD].astype(F32), *tabs, C_ROPE_DIM // 2) * scale).astype(BF16)
          for g in range(heads)]
    qs = [jnp.concatenate([qn[g], qr[g]], axis=1) for g in range(heads)]
    stats = _softmax_stats(scratch_refs)

    def block(start, width, masked):
        rows = pl.ds(start, width)
        kr = kr_ref[0, rows, :]
        scores = [_nt_dot(qs[g], jnp.concatenate([kv_ref[0, rows, 2 * g * HEAD:(2 * g + 1) * HEAD], kr], axis=1))
                  for g in range(heads)]
        if masked:
            mask = _chunk_mask(tq, width, q_start, start)
            scores = [jnp.where(mask, s, NEG_INF) for s in scores]
        values = [kv_ref[0, rows, (2 * g + 1) * HEAD:(2 * g + 2) * HEAD] for g in range(heads)]
        _softmax_steps(scores, values, stats)

    _causal_key_blocks(block, q_start)
    for g in range(heads):
        o_ref[0, :, g * HEAD:(g + 1) * HEAD] = _softmax_result(stats[g]).astype(o_ref.dtype)


def mla_attention(q, kv, kr, q_tables, *, tq, q_off, heads):
    bsz, t, _ = q.shape
    s_len = kv.shape[1]
    _check_tiling(t, tq, q_off, s_len)
    assert C_HEADS % heads == 0
    w = heads * HEAD
    n_groups = C_HEADS // heads
    tab = pl.BlockSpec((tq, HEAD), lambda b, h, i: (i, 0))
    kern = functools.partial(_mla_attn_kernel, tq=tq, q_off=q_off, heads=heads)
    return pl.pallas_call(
        kern,
        grid=(bsz, n_groups, t // tq),
        in_specs=[pl.BlockSpec((1, tq, w), lambda b, h, i: (b, i, h)),
                  pl.BlockSpec((1, tq, w), lambda b, h, i: (b, i, n_groups + h)),
                  pl.BlockSpec((1, s_len, 2 * w), lambda b, h, i: (b, 0, h)),
                  pl.BlockSpec((1, s_len, HEAD), lambda b, h, i: (b, 0, 0)),
                  tab, tab, tab],
        out_specs=pl.BlockSpec((1, tq, w), lambda b, h, i: (b, i, h)),
        out_shape=jax.ShapeDtypeStruct((bsz, t, C_HEADS * HEAD), BF16),
        scratch_shapes=_softmax_scratch(heads, tq, HEAD),
        compiler_params=_params(3),
        name="mla_attention",
    )(q, q, kv, kr, *q_tables)


def _softmax_rows(scores):
    probs, sums = [], []
    for s in scores:
        p = jnp.exp2(s - jnp.max(s, axis=-1, keepdims=True))
        sums.append(jnp.sum(p, axis=-1, keepdims=True))
        probs.append(p.astype(BF16))
    return probs, sums


def _mem_attn_kernel(q_ref, k_ref, v_ref, o_ref):
    col = lambda g: slice(g * HEAD, (g + 1) * HEAD)
    scores = [_nt_dot((q_ref[0, :, col(g)].astype(F32) * (HEAD ** -0.5 * LOG2E)).astype(BF16),
                      k_ref[0, :, col(g)].astype(BF16)) for g in range(MEM_HEADS)]
    probs, sums = _softmax_rows(scores)
    for g in range(MEM_HEADS):
        o = jnp.dot(probs[g], v_ref[0, :, col(g)].astype(BF16), preferred_element_type=F32)
        o_ref[0, :, col(g)] = (o / sums[g]).astype(o_ref.dtype)


def memory_attention(q_src, q_col0, mem_k, mem_v, k_col0, v_col0, *, tq):
    bsz, t, _ = q_src.shape
    n_mem = mem_k.shape[1]
    assert q_col0 % MEM_WIDTH == 0 and k_col0 % MEM_WIDTH == 0 and v_col0 % MEM_WIDTH == 0
    qc, kc, vc = q_col0 // MEM_WIDTH, k_col0 // MEM_WIDTH, v_col0 // MEM_WIDTH
    return pl.pallas_call(
        _mem_attn_kernel,
        grid=(bsz, t // tq),
        in_specs=[pl.BlockSpec((1, tq, MEM_WIDTH), lambda b, i: (b, i, qc)),
                  pl.BlockSpec((1, n_mem, MEM_WIDTH), lambda b, i: (b, 0, kc)),
                  pl.BlockSpec((1, n_mem, MEM_WIDTH), lambda b, i: (b, 0, vc))],
        out_specs=pl.BlockSpec((1, tq, MEM_WIDTH), lambda b, i: (b, i, 0)),
        out_shape=jax.ShapeDtypeStruct((bsz, t, MEM_WIDTH), BF16),
        compiler_params=_params(2),
        name="memory_attention",
    )(q_src, mem_k, mem_v)


def _band_attn_kernel(q_ref, k_ref, v_ref, e_ref, o_ref, bias_ref, *, front_pad, heads):
    i = pl.program_id(2)

    @pl.when(i == 0)
    def _():
        for g in range(heads):
            e = jnp.broadcast_to(e_ref[g] * LOG2E, (BAND_TQ, BAND_TABLE))
            bias_ref[g] = pltpu.roll(e, 0, 1, stride=1, stride_axis=0)[:, :BAND_KEYS]

    rows = pl.ds(pl.multiple_of(i * BAND_TQ, BAND_TQ), BAND_KEYS)
    qq = lax.broadcasted_iota(jnp.int32, (BAND_TQ, BAND_KEYS), 0)
    kk = lax.broadcasted_iota(jnp.int32, (BAND_TQ, BAND_KEYS), 1)
    lo = jnp.maximum(front_pad - i * BAND_TQ, jnp.where(qq < CHUNK, 0, CHUNK))
    hi = jnp.where(qq < CHUNK, BAND_KEYS - CHUNK, BAND_KEYS)
    mask = (kk >= lo) & (kk < hi)
    col = lambda g: slice(g * HEAD, (g + 1) * HEAD)
    scores = [_nt_dot((q_ref[0, :, col(g)].astype(F32) * (HEAD ** -0.5 * LOG2E)).astype(BF16),
                      k_ref[0, rows, col(g)]) for g in range(heads)]
    scores = [jnp.where(mask, s + bias_ref[g], NEG_INF) for g, s in enumerate(scores)]
    probs, sums = _softmax_rows(scores)
    for g in range(heads):
        o = jnp.dot(probs[g], v_ref[0, rows, col(g)], preferred_element_type=F32)
        o_ref[0, :, col(g)] = (o / sums[g]).astype(o_ref.dtype)


def band_attention(q_src, k_pad, v_pad, bias_tab, *, front_pad, heads):
    bsz, t, _ = q_src.shape
    s_len = k_pad.shape[1]
    assert t % BAND_TQ == 0 and s_len == t + B_WINDOW and B_HEADS % heads == 0
    w = heads * HEAD
    kern = functools.partial(_band_attn_kernel, front_pad=front_pad, heads=heads)
    return pl.pallas_call(
        kern,
        grid=(bsz, B_HEADS // heads, t // BAND_TQ),
        in_specs=[pl.BlockSpec((1, BAND_TQ, w), lambda b, h, i: (b, i, h)),
                  pl.BlockSpec((1, s_len, w), lambda b, h, i: (b, 0, h)),
                  pl.BlockSpec((1, s_len, w), lambda b, h, i: (b, 0, h)),
                  pl.BlockSpec((heads, 1, BAND_TABLE), lambda b, h, i: (h, 0, 0))],
        out_specs=pl.BlockSpec((1, BAND_TQ, w), lambda b, h, i: (b, i, h)),
        out_shape=jax.ShapeDtypeStruct((bsz, t, B_WIDTH), BF16),
        scratch_shapes=[pltpu.VMEM((heads, BAND_TQ, BAND_KEYS), F32)],
        compiler_params=_params(3),
        name="band_attention",
    )(q_src, k_pad, v_pad, bias_tab)


def _band_bias_table(rel_bias):
    c = jnp.arange(BAND_TABLE)
    d = jnp.where(c <= BAND_KEYS, c, c - BAND_TABLE)
    idx = jnp.clip(B_WINDOW - d, -B_REL_CLIP, B_REL_CLIP) + B_REL_CLIP
    return rel_bias[:, idx][:, None, :]


def _stick_attn_kernel(q_ref, k_ref, v_ref, o_ref, tri_ref, *state_refs, tq, q_off, heads):
    acc_refs, run_refs = state_refs[:heads], state_refs[heads:]
    i = pl.program_id(2)
    q_start = q_off + i * tq
    tk = ATT_BLK

    @pl.when((pl.program_id(0) == 0) & (pl.program_id(1) == 0) & (i == 0))
    def _():
        r = lax.broadcasted_iota(jnp.int32, (2 * tk, tk), 0)
        c = lax.broadcasted_iota(jnp.int32, (2 * tk, tk), 1)
        tri_ref[...] = jnp.where(jnp.where(r >= tk, r - tk, r) > c, 1.0, 0.0).astype(BF16)

    qs = [(q_ref[0, :, g * HEAD:(g + 1) * HEAD].astype(F32) * (HEAD ** -0.5 * LOG2E)).astype(BF16)
          for g in range(heads)]
    for ref in state_refs:
        ref[...] = jnp.zeros_like(ref)

    def block(start, masked):
        rows = pl.ds(start, tk)
        if masked:
            qpos = q_start + lax.broadcasted_iota(jnp.int32, (tq, tk), 0)
            kpos = start + lax.broadcasted_iota(jnp.int32, (tq, tk), 1)
            allowed = kpos < qpos
        tri = tri_ref[...]
        col = lambda g: slice(g * HEAD, (g + 1) * HEAD)
        zs = [_nt_dot(qs[g], k_ref[0, rows, col(g)]) for g in range(heads)]
        log_betas, log_1m_betas = [], []
        for z in zs:
            log_1m_beta = -(jnp.maximum(z, 0.0) + jnp.log2(1.0 + jnp.exp2(-jnp.abs(z))))
            log_betas.append(z + log_1m_beta)
            log_1m_betas.append(jnp.where(allowed, log_1m_beta, 0.0) if masked else log_1m_beta)
        tails = []
        for g, log_1m_beta in enumerate(log_1m_betas):
            hi = log_1m_beta.astype(BF16)
            lo = (log_1m_beta - hi.astype(F32)).astype(BF16)
            tails.append(jnp.dot(jnp.concatenate([hi, lo], axis=1), tri, preferred_element_type=F32)
                         + run_refs[g][...])
        for g in range(heads):
            a = jnp.exp2(log_betas[g] + tails[g])
            if masked:
                a = jnp.where(allowed, a, 0.0)
            acc_refs[g][...] += jnp.dot(a.astype(BF16), v_ref[0, rows, col(g)], preferred_element_type=F32)
            run_refs[g][...] += jnp.sum(log_1m_betas[g], axis=-1, keepdims=True)

    block(pl.multiple_of(q_start, tk), True)
    n_before = q_start // tk

    def body(step, carry):
        block(pl.multiple_of((n_before - 1 - step) * tk, tk), False)
        return carry

    lax.fori_loop(0, n_before, body, 0)
    for g in range(heads):
        o_ref[0, :, g * HEAD:(g + 1) * HEAD] = acc_refs[g][...].astype(o_ref.dtype)


def stick_attention(src, q_col0, k_rows, k_col0, v_rows, v_col0, *, tq, q_off, heads):
    bsz, t, _ = src.shape
    s_len = k_rows.shape[1]
    _check_tiling(t, tq, q_off, s_len)
    w = heads * HEAD
    assert D_HEADS % heads == 0 and q_col0 % w == 0 and k_col0 % w == 0 and v_col0 % w == 0
    qc, kc, vc = q_col0 // w, k_col0 // w, v_col0 // w
    kern = functools.partial(_stick_attn_kernel, tq=tq, q_off=q_off, heads=heads)
    return pl.pallas_call(
        kern,
        grid=(bsz, D_HEADS // heads, t // tq),
        in_specs=[pl.BlockSpec((1, tq, w), lambda b, h, i: (b, i, qc + h)),
                  pl.BlockSpec((1, s_len, w), lambda b, h, i: (b, 0, kc + h)),
                  pl.BlockSpec((1, s_len, w), lambda b, h, i: (b, 0, vc + h))],
        out_specs=pl.BlockSpec((1, tq, w), lambda b, h, i: (b, i, h)),
        out_shape=jax.ShapeDtypeStruct((bsz, t, D_WIDTH), BF16),
        scratch_shapes=([pltpu.VMEM((2 * ATT_BLK, ATT_BLK), BF16)] + [pltpu.VMEM((tq, HEAD), F32)] * heads
                        + [pltpu.VMEM((tq, 1), F32)] * heads),
        compiler_params=_params(3),
        name="stick_attention",
    )(src, k_rows, v_rows)


def _pad_rows(a, front, back):
    return jnp.pad(a, ((0, 0), (front, back), (0, 0)))


def _round_up(n, mult):
    return (n + mult - 1) // mult * mult


def _with_past_kernel(c_ref, new_ref, o_ref, *, past_blocks):
    r = pl.program_id(1)
    n_heads, head_dim = c_ref.shape[3], c_ref.shape[4]

    @pl.when(r < past_blocks)
    def _():
        for h in range(n_heads):
            o_ref[0, :, h * head_dim:(h + 1) * head_dim] = c_ref[0, 0, :, h, :].astype(BF16)

    @pl.when(r >= past_blocks)
    def _():
        o_ref[0] = jnp.zeros(o_ref.shape[1:], BF16)
        o_ref[0, :new_ref.shape[1], :] = new_ref[0]


def _with_past(cache, layer, new):
    _, bsz, p_len, n_heads, head_dim = cache.shape
    t_new, width = new.shape[1], new.shape[2]
    assert p_len % ATT_BLK == 0 and t_new <= ATT_BLK and width == n_heads * head_dim
    past_blocks = p_len // ATT_BLK
    return pl.pallas_call(
        functools.partial(_with_past_kernel, past_blocks=past_blocks),
        grid=(bsz, past_blocks + 1),
        in_specs=[pl.BlockSpec((1, 1, ATT_BLK, n_heads, head_dim),
                               lambda b, r: (layer, b, jnp.minimum(r, past_blocks - 1), 0, 0)),
                  pl.BlockSpec((1, t_new, width), lambda b, r: (b, 0, 0))],
        out_specs=pl.BlockSpec((1, ATT_BLK, width), lambda b, r: (b, r, 0)),
        out_shape=jax.ShapeDtypeStruct((bsz, p_len + ATT_BLK, width), BF16),
        compiler_params=_params(2),
        name="with_past",
    )(cache, new)


def _mixer_diff(kf, v_rows, zb, pos, past, params, lam_init):
    lq1, lk1, lq2, lk2, norm_g = params
    bsz, t, _ = kf.shape
    tables = _rope_tables(pos, A_ROT_DIM // 2, ROPE_THETA)
    k_rows, k_rows_b = rope_heads(kf, 0, 2 * A_HEADS, tables, A_ROT_DIM // 2, tt=min(t, 512))
    v_col0 = 2 * A_QK_WIDTH
    lam_vecs = jnp.stack([lq1, lq2, lk1, lk2]).astype(F32)
    g = norm_g.reshape(1, 2 * HEAD).astype(F32)
    if past is None:
        o = diff_attention(zb, k_rows_b, zb, v_col0, tables, lam_vecs, g, tq=ATT_TQ, q_off=0,
                           lam_init=lam_init, heads=2)
    else:
        k_all = _with_past(past[0][None], 0, k_rows_b)
        v_all = _with_past(past[1][None], 0, zb[:, :, v_col0:v_col0 + MIX_WIDTH])
        o = diff_attention(zb, k_all, v_all, 0, tables, lam_vecs, g, tq=t, q_off=past[0].shape[1],
                           lam_init=lam_init, heads=3)
    shape = (bsz, t, A_HEADS, 2 * HEAD)
    return o, (k_rows.reshape(shape), v_rows.reshape(shape))


def _mixer_band(k, v, zb, pos, past, params):
    (rel_bias,) = params
    bsz, t, _ = k.shape
    kb = zb[:, :, B_WIDTH:2 * B_WIDTH]
    vb = zb[:, :, 2 * B_WIDTH:3 * B_WIDTH]
    bias_tab = _band_bias_table(rel_bias.astype(F32))
    shape = (bsz, -1, B_HEADS, HEAD)
    if past is None:
        o = band_attention(zb, _pad_rows(kb, B_WINDOW, 0), _pad_rows(vb, B_WINDOW, 0), bias_tab,
                           front_pad=B_WINDOW, heads=4)
        keep = min(B_WINDOW, t)
        state = (k[:, t - keep:].reshape(shape), v[:, t - keep:].reshape(shape))
    else:
        buf_len = past[0].shape[1]
        assert t == CHUNK and buf_len == B_WINDOW
        k_all = jnp.concatenate([past[0].reshape(bsz, buf_len, B_WIDTH), k], axis=1)
        v_all = jnp.concatenate([past[1].reshape(bsz, buf_len, B_WIDTH), v], axis=1)
        q_pad = _pad_rows(zb[:, :, :B_WIDTH], BAND_TQ - t, 0)
        o = band_attention(q_pad, _pad_rows(k_all.astype(BF16), CHUNK, 0), _pad_rows(v_all.astype(BF16), CHUNK, 0),
                           bias_tab, front_pad=CHUNK, heads=6)[:, BAND_TQ - t:]
        state = (k_all[:, t:].reshape(shape), v_all[:, t:].reshape(shape))
    return o, state


def _mixer_mla(z, pos, past, params, w_uq, w_ukv):
    q_norm_g, kv_norm_g = params
    bsz, t, _ = z.shape
    half = C_ROPE_DIM // 2
    tables = _rope_tables(pos, half, C_ROPE_THETA)
    cq, latent, kr = mla_prep(z.reshape(bsz * t, z.shape[-1]), q_norm_g.reshape(1, -1).astype(F32),
                              kv_norm_g.reshape(1, -1).astype(F32), tables, seq_len=t, tt=min(t, 512))
    q = matmul(cq, w_uq, tm=min(bsz * t, 1024), tn=1024, out_dtypes=(BF16,)).reshape(bsz, t, -1)
    latent = latent.reshape(bsz, t, C_KV_RANK)
    kr = kr.reshape(bsz, t, HEAD)
    if past is None:
        lat_all, kr_all, q_off, tq, heads = latent, kr.astype(BF16), 0, ATT_TQ, 4
    else:
        q_off, tq, heads = past[0].shape[1], t, 6
        back = _round_up(q_off + t, ATT_BLK) - q_off - t
        lat_all = _pad_rows(jnp.concatenate([past[0], latent], axis=1), 0, back)
        kr_past = jnp.pad(past[1], ((0, 0), (0, 0), (0, HEAD - C_ROPE_DIM)))
        kr_all = _pad_rows(jnp.concatenate([kr_past, kr], axis=1), 0, back).astype(BF16)
    s_len = lat_all.shape[1]
    kv = matmul(lat_all.reshape(bsz * s_len, C_KV_RANK), w_ukv, tm=_pick(bsz * s_len, (1024, 512, 256)),
                tn=1024, out_dtypes=(BF16,)).reshape(bsz, s_len, -1)
    o = mla_attention(q, kv, kr_all, tables, tq=tq, q_off=q_off, heads=heads)
    return o, (latent, kr[:, :, :C_ROPE_DIM])


def _mixer_stick(k, v, zb, pos, past):
    bsz, t, _ = k.shape
    if past is None:
        o = stick_attention(zb, 0, zb, D_WIDTH, zb, 2 * D_WIDTH, tq=ATT_TQ, q_off=0, heads=4)
    else:
        k_all = _with_past(past[0][None], 0, zb[:, :, D_WIDTH:2 * D_WIDTH])
        v_all = _with_past(past[1][None], 0, zb[:, :, 2 * D_WIDTH:3 * D_WIDTH])
        o = stick_attention(zb, 0, k_all, 0, v_all, 0, tq=t, q_off=past[0].shape[1], heads=6)
    shape = (bsz, t, D_HEADS, HEAD)
    return o, (k.reshape(shape), v.reshape(shape))


def _trunk_layer(layer, x, pos, past, mem_kv, conv_state, mix_params, w):
    bsz, t, _ = x.shape
    m = bsz * t
    mixer = layer % N_MIXERS
    x2d = x.reshape(m, D_MODEL)
    z, zb = matmul(x2d, w["w_in"], tm=min(m, 1024), tn=1024, out_dtypes=(F32, BF16))
    z = z.reshape(bsz, t, -1)
    zb = zb.reshape(bsz, t, -1)
    zf = [z] if mixer == 2 else [z[:, :, MIX_WIDTH:2 * MIX_WIDTH], z[:, :, 2 * MIX_WIDTH:3 * MIX_WIDTH]]
    if mixer == 0:
        o_mix, state = _mixer_diff(*zf, zb, pos, past, mix_params, 0.8 - 0.6 * math.exp(-0.3 * layer))
    elif mixer == 1:
        o_mix, state = _mixer_band(*zf, zb, pos, past, mix_params)
    elif mixer == 2:
        o_mix, state = _mixer_mla(*zf, pos, past, mix_params, w["w_uq"], w["w_ukv"])
    else:
        o_mix, state = _mixer_stick(*zf, zb, pos, past)
    q_mem_col0 = C_MEM_COL if mixer == 2 else 3 * MIX_WIDTH
    o_mem = memory_attention(zb, q_mem_col0, mem_kv[0], mem_kv[1], mem_kv[2], mem_kv[3], tq=min(t, 512))
    x1 = proj_ln([o_mix.reshape(m, MIX_WIDTH), o_mem.reshape(m, MEM_WIDTH)], w["w_o"], x2d,
                 w["ln1_g"], w["ln1_b"], tm=512, name="out_proj_ln")
    g = ffn_up(x1, w["w_up"], w["conv_w"], w["conv_b"], conv_state, seq_len=t, tm=512, tn=512)
    x2 = proj_ln([g], w["w_down"], x1, w["ln2_g"], w["ln2_b"], tm=256, name="down_proj_ln")
    return x2.reshape(bsz, t, D_MODEL), state, x1.reshape(bsz, t, D_MODEL)


def _reorder_w_in_c(w):
    a, b = C_KR_COL, C_KR_COL + C_ROPE_DIM
    pad = jnp.zeros((w.shape[0], C_MEM_COL - b), w.dtype)
    return jnp.concatenate([w[:, :b], pad, w[:, b:]], axis=1)


def _reorder_w_uq(w):
    w = w.reshape(C_Q_RANK, C_HEADS, C_NOPE_DIM + C_ROPE_DIM)
    nope = w[:, :, :C_NOPE_DIM].reshape(C_Q_RANK, C_HEADS * HEAD)
    rope = jnp.pad(w[:, :, C_NOPE_DIM:], ((0, 0), (0, 0), (0, HEAD - C_ROPE_DIM)))
    return jnp.concatenate([nope, rope.reshape(C_Q_RANK, C_HEADS * HEAD)], axis=1)


def kernel(x_prompt, x_sample, mem_prompt, cache_a_k, cache_a_v, cache_b_k, cache_b_v, cache_c_latent, cache_c_krope, cache_d_k, cache_d_v, cache_mem_k, cache_mem_v, state_ffn_conv, w_in_a, w_in_b, w_in_c, w_in_d, diff_lambda_q1, diff_lambda_k1, diff_lambda_q2, diff_lambda_k2, diff_norm_g, band_rel_bias, mla_q_norm_g, mla_kv_norm_g, mla_w_uq, mla_w_ukv, w_mem_kv, w_o, ln1_g, ln1_b, w_up, conv_ffn_w, conv_ffn_b, w_down, ln2_g, ln2_b):
    n_p, t_p, _ = x_prompt.shape
    n_s, t_s, _ = x_sample.shape
    past_len = cache_d_k.shape[2]
    pos_p = jnp.arange(t_p)
    pos_s = past_len + jnp.arange(t_s)
    w_in_by_type = (w_in_a, w_in_b, w_in_c, w_in_d)
    caches_by_type = ((cache_a_k, cache_a_v), (cache_b_k, cache_b_v),
                      (cache_c_latent, cache_c_krope), (cache_d_k, cache_d_v))
    params_by_type = ((diff_lambda_q1, diff_lambda_k1, diff_lambda_q2, diff_lambda_k2, diff_norm_g),
                      (band_rel_bias,), (mla_q_norm_g, mla_kv_norm_g), ())
    states_p = [([], []) for _ in range(N_MIXERS)]
    states_s = [([], []) for _ in range(N_MIXERS)]
    mem_k_p, mem_v_p, conv_p, conv_s = [], [], [], []
    n_mem = mem_prompt.shape[1]
    mem2d = mem_prompt.reshape(n_p * n_mem, D_MODEL)
    zero_state = jnp.zeros((n_p, 2, 2 * D_FF), F32)
    x_p, x_s = x_prompt, x_sample
    for i in range(DEPTH):
        mixer, j = i % N_MIXERS, i // N_MIXERS
        w_in = w_in_by_type[mixer][j]
        w = {
            "w_in": (_reorder_w_in_c(w_in) if mixer == 2 else w_in).astype(BF16),
            "w_o": w_o[i].astype(BF16),
            "ln1_g": ln1_g[i].reshape(1, -1), "ln1_b": ln1_b[i].reshape(1, -1),
            "w_up": w_up[i].astype(BF16),
            "conv_w": conv_ffn_w[i], "conv_b": conv_ffn_b[i].reshape(1, -1),
            "w_down": w_down[i].astype(BF16),
            "ln2_g": ln2_g[i].reshape(1, -1), "ln2_b": ln2_b[i].reshape(1, -1),
        }
        if mixer == 2:
            w["w_uq"] = _reorder_w_uq(mla_w_uq[j]).astype(BF16)
            w["w_ukv"] = mla_w_ukv[j].astype(BF16)
        mix_params = tuple(p[j] for p in params_by_type[mixer])
        kv_mem, kv_mem_b = matmul(mem2d, w_mem_kv[i].astype(BF16), tm=n_p * n_mem, tn=512, out_dtypes=(F32, BF16))
        kv_mem = kv_mem.reshape(n_p, n_mem, 2 * MEM_WIDTH)
        kv_mem_b = kv_mem_b.reshape(n_p, n_mem, 2 * MEM_WIDTH)
        x_p, st_p, x1_p = _trunk_layer(i, x_p, pos_p, None, (kv_mem_b, kv_mem_b, 0, MEM_WIDTH), zero_state,
                                       mix_params, w)
        past = (caches_by_type[mixer][0][j], caches_by_type[mixer][1][j])
        mem_s = (cache_mem_k[i].reshape(n_s, n_mem, MEM_WIDTH), cache_mem_v[i].reshape(n_s, n_mem, MEM_WIDTH), 0, 0)
        x_s, st_s, x1_s = _trunk_layer(i, x_s, pos_s, past, mem_s, state_ffn_conv[i], mix_params, w)
        tails = jnp.concatenate([x1_p[:, t_p - 2:].reshape(2 * n_p, D_MODEL),
                                 x1_s[:, t_s - 2:].reshape(2 * n_s, D_MODEL)], axis=0)
        n_tail = tails.shape[0]
        tails = jnp.pad(tails, ((0, _round_up(n_tail, 16) - n_tail), (0, 0)))
        u_tail = matmul(tails, w["w_up"], tm=tails.shape[0], tn=1024)
        conv_p.append(u_tail[:2 * n_p].reshape(n_p, 2, 2 * D_FF))
        conv_s.append(u_tail[2 * n_p:n_tail].reshape(n_s, 2, 2 * D_FF))
        for a in range(2):
            states_p[mixer][a].append(st_p[a])
            states_s[mixer][a].append(st_s[a])
        mem_k_p.append(kv_mem[:, :, :MEM_WIDTH].reshape(n_p, n_mem, MEM_HEADS, HEAD))
        mem_v_p.append(kv_mem[:, :, MEM_WIDTH:].reshape(n_p, n_mem, MEM_HEADS, HEAD))
    outs = [x_p, x_s]
    for st in states_p:
        outs += [jnp.stack(st[0]), jnp.stack(st[1])]
    outs += [jnp.stack(mem_k_p), jnp.stack(mem_v_p), jnp.stack(conv_p)]
    for st in states_s:
        outs += [jnp.stack(st[0]), jnp.stack(st[1])]
    outs.append(jnp.stack(conv_s))
    return tuple(outs)
```

```python
import functools
import math

import jax
import jax.numpy as jnp
from jax import lax
from jax.experimental import pallas as pl
from jax.experimental.pallas import tpu as pltpu

F32 = jnp.float32
BF16 = jnp.bfloat16

D_MODEL = 2048
DEPTH = 4
CHUNK = 64
CHUNK_SHIFT = 6
N_MIXERS = 4
MIX_WIDTH = 1536
MEM_HEADS = 4
MEM_WIDTH = 512
HEAD = 128
ROPE_THETA = 500000.0

A_HEADS = 6
A_QK_WIDTH = 1536
A_ROT_DIM = 32

B_HEADS = 12
B_WIDTH = 1536
B_WINDOW = 512
B_REL_CLIP = 128
BAND_TQ = 128
BAND_KEYS = B_WINDOW + BAND_TQ
BAND_TABLE = 768

C_HEADS = 12
C_Q_RANK = 768
C_KV_RANK = 512
C_NOPE_DIM = 128
C_ROPE_DIM = 64
C_ROPE_THETA = 10000.0
C_KR_COL = C_Q_RANK + C_KV_RANK
C_MEM_COL = 1536
C_IN_PAD = C_MEM_COL + MEM_WIDTH

D_HEADS = 12
D_WIDTH = 1536

D_FF = 5632
FFN_HALO = 16

ATT_TQ = 256
ATT_BLK = 256
ATT_WIDE = 512

DEEPNORM_ALPHA = (2 * DEPTH) ** 0.25
NORM_EPS = 1e-5
NEG_INF = -1e30
LOG2E = 1.0 / math.log(2.0)

VMEM_LIMIT = 56 * 1024 * 1024


def _params(n_axes):
    return pltpu.CompilerParams(dimension_semantics=("arbitrary",) * n_axes,
                                vmem_limit_bytes=VMEM_LIMIT)


def _pick(n, cands):
    for c in cands:
        if n % c == 0:
            return c
    return n


def _mm_kernel(a_ref, b_ref, *rest):
    *o_refs, abf_ref = rest

    @pl.when(pl.program_id(1) == 0)
    def _():
        abf_ref[...] = a_ref[...].astype(BF16)

    r = jnp.dot(abf_ref[...], b_ref[...], preferred_element_type=F32)
    for o_ref in o_refs:
        o_ref[...] = r.astype(o_ref.dtype)


def _layer_weight(w):
    if isinstance(w, tuple):
        return w[0], (None,), (w[1],)
    return w, (), ()


def matmul(a, b, *, tm, tn, out_dtypes=(F32,)):
    m, k = a.shape
    b, b_blk, b_idx = _layer_weight(b)
    n = b.shape[-1]
    assert m % tm == 0 and n % tn == 0, (a.shape, b.shape, tm, tn)
    outs = pl.pallas_call(
        _mm_kernel,
        grid=(m // tm, n // tn),
        in_specs=[pl.BlockSpec((tm, k), lambda i, j: (i, 0)),
                  pl.BlockSpec(b_blk + (k, tn), lambda i, j: b_idx + (0, j))],
        out_specs=[pl.BlockSpec((tm, tn), lambda i, j: (i, j)) for _ in out_dtypes],
        out_shape=[jax.ShapeDtypeStruct((m, n), dt) for dt in out_dtypes],
        scratch_shapes=[pltpu.VMEM((tm, k), BF16)],
        compiler_params=_params(2),
        name="matmul",
    )(a, b)
    return outs[0] if len(out_dtypes) == 1 else outs


def _layer_norm_rows(y, g, b):
    mu = jnp.mean(y, axis=-1, keepdims=True)
    d = y - mu
    var = jnp.mean(d * d, axis=-1, keepdims=True)
    return d * lax.rsqrt(var + NORM_EPS) * g + b


def _proj_ln_kernel(*refs, n_parts):
    a_refs = refs[:n_parts]
    w_ref, x_ref, g_ref, b_ref, o_ref = refs[n_parts:]
    f, k0 = None, 0
    for a_ref in a_refs:
        k = a_ref.shape[1]
        part = jnp.dot(a_ref[...], w_ref[k0:k0 + k, :], preferred_element_type=F32)
        f = part if f is None else f + part
        k0 += k
    y = DEEPNORM_ALPHA * x_ref[...] + f
    o_ref[...] = _layer_norm_rows(y, g_ref[...], b_ref[...])


def proj_ln(a_parts, w, x, g, b, *, tm, name):
    m = x.shape[0]
    w, w_blk, w_idx = _layer_weight(w)
    k_total, n = w.shape[-2:]
    assert sum(a.shape[1] for a in a_parts) == k_total and m % tm == 0
    row = lambda i: (i, 0)
    fixed = lambda i: (0, 0)
    return pl.pallas_call(
        functools.partial(_proj_ln_kernel, n_parts=len(a_parts)),
        grid=(m // tm,),
        in_specs=([pl.BlockSpec((tm, a.shape[1]), row) for a in a_parts]
                  + [pl.BlockSpec(w_blk + (k_total, n), lambda i: w_idx + (0, 0), pipeline_mode=pl.Buffered(1)),
                     pl.BlockSpec((tm, n), row), pl.BlockSpec((1, n), fixed), pl.BlockSpec((1, n), fixed)]),
        out_specs=pl.BlockSpec((tm, n), row),
        out_shape=jax.ShapeDtypeStruct((m, n), F32),
        compiler_params=_params(1),
        name=name,
    )(*a_parts, w, x, g, b)


def _silu(x):
    return x * (1.0 / (1.0 + jnp.exp(-x)))


def _ffn_up_kernel(x_ref, xh_ref, wg_ref, wv_ref, cwg_ref, cwv_ref, cbg_ref, cbv_ref,
                   sg_ref, sv_ref, o_ref, xcat_ref, *, tiles_per_seq, seq_starts):
    i = pl.program_id(0)
    j = pl.program_id(1)

    @pl.when(j == 0)
    def _():
        xcat_ref[:FFN_HALO, :] = xh_ref[...].astype(BF16)
        xcat_ref[FFN_HALO:, :] = x_ref[...].astype(BF16)

    def conv(u, cw_ref, cb_ref):
        cw = cw_ref[...]
        p1 = pltpu.roll(u, 1, 0)[FFN_HALO:]
        p2 = pltpu.roll(u, 2, 0)[FFN_HALO:]
        return cb_ref[...] + cw[0:1] * p2 + cw[1:2] * p1 + cw[2:3] * u[FFN_HALO:]

    xe = xcat_ref[...]
    ug = jnp.dot(xe, wg_ref[...], preferred_element_type=F32)
    uv = jnp.dot(xe, wv_ref[...], preferred_element_type=F32)
    o_ref[...] = (_silu(conv(ug, cwg_ref, cbg_ref)) * conv(uv, cwv_ref, cbv_ref)).astype(o_ref.dtype)

    def fix_sequence_start(seq, row0):
        rid = lax.broadcasted_iota(jnp.int32, (FFN_HALO, o_ref.shape[1]), 0)

        def conv_head(u, cw_ref, cb_ref, st_ref):
            cw = cw_ref[...]
            cur = u[FFN_HALO + row0:2 * FFN_HALO + row0]
            st = st_ref[seq]
            s0, s1 = st[0:1], st[1:2]
            p1 = jnp.where(rid == 0, s1, pltpu.roll(cur, 1, 0))
            p2 = jnp.where(rid == 0, s0, jnp.where(rid == 1, s1, pltpu.roll(cur, 2, 0)))
            return cb_ref[...] + cw[0:1] * p2 + cw[1:2] * p1 + cw[2:3] * cur

        hg = conv_head(ug, cwg_ref, cbg_ref, sg_ref)
        hv = conv_head(uv, cwv_ref, cbv_ref, sv_ref)
        o_ref[row0:row0 + FFN_HALO, :] = (_silu(hg) * hv).astype(o_ref.dtype)

    if tiles_per_seq > 1:
        pl.when(i % tiles_per_seq == 0)(lambda: fix_sequence_start(0, 0))
    else:
        for seq, row0 in enumerate(seq_starts):
            fix_sequence_start(seq, row0)


def ffn_up(x, w_up, conv_w, conv_b, state, *, seq_len, tm, tn):
    m, k = x.shape
    w_up, w_blk, w_idx = _layer_weight(w_up)
    assert (seq_len % tm == 0 or tm % seq_len == 0) and seq_len % FFN_HALO == 0 and D_FF % tn == 0
    tiles_per_seq = max(seq_len // tm, 1)
    seqs_per_tile = max(tm // seq_len, 1)
    nf = D_FF // tn
    halo_blocks = tm // FFN_HALO
    kern = functools.partial(_ffn_up_kernel, tiles_per_seq=tiles_per_seq,
                             seq_starts=tuple(q * seq_len for q in range(seqs_per_tile)))
    seq_block = lambda i: i * seqs_per_tile // tiles_per_seq // seqs_per_tile
    return pl.pallas_call(
        kern,
        grid=(m // tm, nf),
        in_specs=[
            pl.BlockSpec((tm, k), lambda i, j: (i, 0)),
            pl.BlockSpec((FFN_HALO, k), lambda i, j: (jnp.maximum(i * halo_blocks - 1, 0), 0)),
            pl.BlockSpec(w_blk + (k, tn), lambda i, j: w_idx + (0, j)),
            pl.BlockSpec(w_blk + (k, tn), lambda i, j: w_idx + (0, j + nf)),
            pl.BlockSpec((3, tn), lambda i, j: (0, j)),
            pl.BlockSpec((3, tn), lambda i, j: (0, j + nf)),
            pl.BlockSpec((1, tn), lambda i, j: (0, j)),
            pl.BlockSpec((1, tn), lambda i, j: (0, j + nf)),
            pl.BlockSpec((seqs_per_tile, 2, tn), lambda i, j: (seq_block(i), 0, j)),
            pl.BlockSpec((seqs_per_tile, 2, tn), lambda i, j: (seq_block(i), 0, j + nf)),
        ],
        out_specs=pl.BlockSpec((tm, tn), lambda i, j: (i, j)),
        out_shape=jax.ShapeDtypeStruct((m, D_FF), BF16),
        scratch_shapes=[pltpu.VMEM((tm + FFN_HALO, k), BF16)],
        compiler_params=_params(2),
        name="ffn_up",
    )(x, x, w_up, w_up, conv_w, conv_w, conv_b, conv_b, state, state)


def _rope_tables(pos, half, theta):
    inv_freq = theta ** (-jnp.arange(half, dtype=F32) / half)
    ang = pos.astype(F32)[:, None] * inv_freq[None, :]
    cos, sin = jnp.cos(ang), jnp.sin(ang)
    t = pos.shape[0]
    rest = HEAD - 2 * half
    cos_t = jnp.concatenate([cos, cos, jnp.ones((t, rest), F32)], axis=1)
    sin_up = jnp.concatenate([jnp.zeros((t, half), F32), sin, jnp.zeros((t, rest), F32)], axis=1)
    sin_dn = jnp.concatenate([-sin, jnp.zeros((t, half + rest), F32)], axis=1)
    return cos_t, sin_up, sin_dn


def _rope(x, cos_t, sin_up, sin_dn, half):
    return x * cos_t + pltpu.roll(x, half, 1) * sin_up + pltpu.roll(x, HEAD - half, 1) * sin_dn


def _rope_heads_kernel(x_ref, c_ref, su_ref, sd_ref, o_ref, ob_ref, *, half, n_heads):
    tabs = (c_ref[...], su_ref[...], sd_ref[...])
    for h in range(n_heads):
        cols = slice(h * HEAD, (h + 1) * HEAD)
        r = _rope(x_ref[0, :, cols], *tabs, half)
        o_ref[0, :, cols] = r
        ob_ref[0, :, cols] = r.astype(BF16)


def rope_heads(z, col0, n_heads, tables, half, *, tt):
    bsz, t, _ = z.shape
    w = n_heads * HEAD
    assert col0 % w == 0
    c0 = col0 // w
    tab = pl.BlockSpec((tt, HEAD), lambda b, i: (i, 0))
    out = pl.BlockSpec((1, tt, w), lambda b, i: (b, i, 0))
    return pl.pallas_call(
        functools.partial(_rope_heads_kernel, half=half, n_heads=n_heads),
        grid=(bsz, t // tt),
        in_specs=[pl.BlockSpec((1, tt, w), lambda b, i: (b, i, c0)), tab, tab, tab],
        out_specs=[out, out],
        out_shape=[jax.ShapeDtypeStruct((bsz, t, w), F32), jax.ShapeDtypeStruct((bsz, t, w), BF16)],
        compiler_params=_params(2),
        name="rope_heads",
    )(z, *tables)


def _rms_rows(x, g):
    return x * lax.rsqrt(jnp.mean(x * x, axis=-1, keepdims=True) + NORM_EPS) * g


def _mla_prep_kernel(z_ref, gq_ref, gkv_ref, c_ref, su_ref, sd_ref, cq_ref, lat_ref, kr_ref):
    z = z_ref[...]
    cq_ref[...] = _rms_rows(z[:, :C_Q_RANK], gq_ref[...]).astype(cq_ref.dtype)
    lat_ref[...] = _rms_rows(z[:, C_Q_RANK:C_KR_COL], gkv_ref[...])
    kr = z[:, C_KR_COL:C_KR_COL + HEAD]
    kr_ref[...] = _rope(kr, c_ref[...], su_ref[...], sd_ref[...], C_ROPE_DIM // 2)


def mla_prep(z2d, gq, gkv, tables, *, seq_len, tt):
    m = z2d.shape[0]
    nt = seq_len // tt
    row = lambda i: (i, 0)
    fixed = lambda i: (0, 0)
    tab = pl.BlockSpec((tt, HEAD), lambda i: (i % nt, 0))
    return pl.pallas_call(
        _mla_prep_kernel,
        grid=(m // tt,),
        in_specs=[pl.BlockSpec((tt, C_MEM_COL), row), pl.BlockSpec((1, C_Q_RANK), fixed),
                  pl.BlockSpec((1, C_KV_RANK), fixed), tab, tab, tab],
        out_specs=[pl.BlockSpec((tt, C_Q_RANK), row), pl.BlockSpec((tt, C_KV_RANK), row),
                   pl.BlockSpec((tt, HEAD), row)],
        out_shape=[jax.ShapeDtypeStruct((m, C_Q_RANK), BF16),
                   jax.ShapeDtypeStruct((m, C_KV_RANK), F32),
                   jax.ShapeDtypeStruct((m, HEAD), F32)],
        compiler_params=_params(1),
        name="mla_prep",
    )(z2d, gq, gkv, *tables)


def _nt_dot(a, b):
    return lax.dot_general(a, b, (((1,), (1,)), ((), ())), preferred_element_type=F32)


def _lanes(x, n):
    return x if n == HEAD else jnp.concatenate([x] * (n // HEAD), axis=1)


def _chunk_mask(tq, tk, q_start, k_start):
    qpos = q_start + lax.broadcasted_iota(jnp.int32, (tq, tk), 0)
    kpos = k_start + lax.broadcasted_iota(jnp.int32, (tq, tk), 1)
    return jnp.right_shift(kpos, CHUNK_SHIFT) <= jnp.right_shift(qpos, CHUNK_SHIFT)


def _softmax_steps(scores, values, stats):
    probs, alphas = [], []
    for s, (m_ref, l_ref, _) in zip(scores, stats):
        tk = s.shape[1]
        m_old = m_ref[...]
        m_new = jnp.maximum(m_old, jnp.max(s, axis=-1, keepdims=True))
        alpha = jnp.exp2(m_old - m_new)
        p = jnp.exp2(s - _lanes(m_new, tk))
        psum = p[:, :HEAD]
        for c in range(1, tk // HEAD):
            psum = psum + p[:, c * HEAD:(c + 1) * HEAD]
        l_ref[...] = alpha * l_ref[...] + psum
        m_ref[...] = m_new
        probs.append(p.astype(BF16))
        alphas.append(alpha)
    for p, v, alpha, (_, _, acc_ref) in zip(probs, values, alphas, stats):
        acc_ref[...] = (_lanes(alpha, acc_ref.shape[-1]) * acc_ref[...]
                        + jnp.dot(p, v, preferred_element_type=F32))


def _softmax_scratch(n_chains, tq, dv):
    return [pltpu.VMEM((tq, HEAD), F32), pltpu.VMEM((tq, HEAD), F32), pltpu.VMEM((tq, dv), F32)] * n_chains


def _softmax_stats(scratch_refs):
    stats = [tuple(scratch_refs[3 * c:3 * c + 3]) for c in range(len(scratch_refs) // 3)]
    for m_ref, l_ref, acc_ref in stats:
        m_ref[...] = jnp.full_like(m_ref, NEG_INF)
        l_ref[...] = jnp.zeros_like(l_ref)
        acc_ref[...] = jnp.zeros_like(acc_ref)
    return stats


def _softmax_result(stat):
    _, l_ref, acc_ref = stat
    return acc_ref[...] / jnp.sum(l_ref[...], axis=-1, keepdims=True)


def _causal_key_blocks(block, q_start):
    n_wide = q_start // ATT_WIDE

    def body(j, carry):
        block(pl.multiple_of(j * ATT_WIDE, ATT_WIDE), ATT_WIDE, False)
        return carry

    lax.fori_loop(0, n_wide, body, 0)
    rest = n_wide * ATT_WIDE

    @pl.when(q_start - rest >= ATT_BLK)
    def _():
        block(pl.multiple_of(rest, ATT_BLK), ATT_BLK, False)

    block(pl.multiple_of(q_start, ATT_BLK), ATT_BLK, True)


def _check_tiling(t, tq, q_off, s_len):
    assert t % tq == 0 and q_off % ATT_BLK == 0 and s_len % ATT_BLK == 0
    assert tq == ATT_BLK or (t == tq and tq <= ATT_BLK), "own chunks must sit in one ATT_BLK key block"
    assert q_off + t <= s_len


def _diff_attn_kernel(q_ref, k_ref, v_ref, c_ref, su_ref, sd_ref, lam_ref, g_ref, o_ref,
                      *scratch_refs, tq, q_off, lam_init, heads):
    i = pl.program_id(2)
    q_start = q_off + i * tq
    scale = HEAD ** -0.5 * LOG2E
    tabs = (c_ref[...], su_ref[...], sd_ref[...])
    n_slots = 2 * heads
    qs = [(_rope(q_ref[0, :, sl * HEAD:(sl + 1) * HEAD].astype(F32), *tabs, A_ROT_DIM // 2) * scale).astype(BF16)
          for sl in range(n_slots)]
    stats = _softmax_stats(scratch_refs)

    def block(start, width, masked):
        rows = pl.ds(start, width)
        scores = [_nt_dot(qs[sl], k_ref[0, rows, sl * HEAD:(sl + 1) * HEAD]) for sl in range(n_slots)]
        if masked:
            mask = _chunk_mask(tq, width, q_start, start)
            scores = [jnp.where(mask, s, NEG_INF) for s in scores]
        values = [v_ref[0, rows, (sl // 2) * 2 * HEAD:(sl // 2 + 1) * 2 * HEAD] for sl in range(n_slots)]
        _softmax_steps(scores, values, stats)

    _causal_key_blocks(block, q_start)

    lam_v = lam_ref[...]
    dots = jnp.sum(lam_v[0:2] * lam_v[2:4], axis=-1, keepdims=True)
    lam = jnp.exp(dots[0:1]) - jnp.exp(dots[1:2]) + lam_init
    for g in range(heads):
        o = _softmax_result(stats[2 * g]) - lam * _softmax_result(stats[2 * g + 1])
        o_ref[0, :, g * 2 * HEAD:(g + 1) * 2 * HEAD] = (
            _rms_rows(o, g_ref[...]) * (1.0 - lam_init)).astype(o_ref.dtype)


def diff_attention(q_src, k_rows, v_src, v_col0, q_tables, lam_vecs, norm_g, *, tq, q_off, lam_init, heads):
    bsz, t, _ = q_src.shape
    s_len = k_rows.shape[1]
    _check_tiling(t, tq, q_off, s_len)
    w = heads * 2 * HEAD
    assert A_HEADS % heads == 0 and v_col0 % w == 0
    vc = v_col0 // w
    tab = pl.BlockSpec((tq, HEAD), lambda b, h, i: (i, 0))
    kern = functools.partial(_diff_attn_kernel, tq=tq, q_off=q_off, lam_init=lam_init, heads=heads)
    return pl.pallas_call(
        kern,
        grid=(bsz, A_HEADS // heads, t // tq),
        in_specs=[pl.BlockSpec((1, tq, w), lambda b, h, i: (b, i, h)),
                  pl.BlockSpec((1, s_len, w), lambda b, h, i: (b, 0, h)),
                  pl.BlockSpec((1, s_len, w), lambda b, h, i: (b, 0, vc + h)),
                  tab, tab, tab,
                  pl.BlockSpec((4, HEAD), lambda b, h, i: (0, 0)),
                  pl.BlockSpec((1, 2 * HEAD), lambda b, h, i: (0, 0))],
        out_specs=pl.BlockSpec((1, tq, w), lambda b, h, i: (b, i, h)),
        out_shape=jax.ShapeDtypeStruct((bsz, t, A_HEADS * 2 * HEAD), BF16),
        scratch_shapes=_softmax_scratch(2 * heads, tq, 2 * HEAD),
        compiler_params=_params(3),
        name="diff_attention",
    )(q_src, k_rows, v_src, *q_tables, lam_vecs, norm_g)


def _mla_attn_kernel(qn_ref, qr_ref, kv_ref, kr_ref, c_ref, su_ref, sd_ref, o_ref,
                     *scratch_refs, tq, q_off, heads):
    i = pl.program_id(2)
    q_start = q_off + i * tq
    scale = (C_NOPE_DIM + C_ROPE_DIM) ** -0.5 * LOG2E
    tabs = (c_ref[...], su_ref[...], sd_ref[...])
    qn = [(qn_ref[0, :, g * HEAD:(g + 1) * HEAD].astype(F32) * scale).astype(BF16) for g in range(heads)]
    qr = [(_rope(qr_ref[0, :, g * HEAD:(g + 1) * HEAD].astype(F32), *tabs, C_ROPE_DIM // 2) * scale).astype(BF16)
          for g in range(heads)]
    qs = [jnp.concatenate([qn[g], qr[g]], axis=1) for g in range(heads)]
    stats = _softmax_stats(scratch_refs)

    def block(start, width, masked):
        rows = pl.ds(start, width)
        kr = kr_ref[0, rows, :]
        scores = [_nt_dot(qs[g], jnp.concatenate([kv_ref[0, rows, 2 * g * HEAD:(2 * g + 1) * HEAD], kr], axis=1))
                  for g in range(heads)]
        if masked:
            mask = _chunk_mask(tq, width, q_start, start)
            scores = [jnp.where(mask, s, NEG_INF) for s in scores]
        values = [kv_ref[0, rows, (2 * g + 1) * HEAD:(2 * g + 2) * HEAD] for g in range(heads)]
        _softmax_steps(scores, values, stats)

    _causal_key_blocks(block, q_start)
    for g in range(heads):
        o_ref[0, :, g * HEAD:(g + 1) * HEAD] = _softmax_result(stats[g]).astype(o_ref.dtype)


def mla_attention(q, kv, kr, q_tables, *, tq, q_off, heads):
    bsz, t, _ = q.shape
    s_len = kv.shape[1]
    _check_tiling(t, tq, q_off, s_len)
    assert C_HEADS % heads == 0
    w = heads * HEAD
    n_groups = C_HEADS // heads
    tab = pl.BlockSpec((tq, HEAD), lambda b, h, i: (i, 0))
    kern = functools.partial(_mla_attn_kernel, tq=tq, q_off=q_off, heads=heads)
    return pl.pallas_call(
        kern,
        grid=(bsz, n_groups, t // tq),
        in_specs=[pl.BlockSpec((1, tq, w), lambda b, h, i: (b, i, h)),
                  pl.BlockSpec((1, tq, w), lambda b, h, i: (b, i, n_groups + h)),
                  pl.BlockSpec((1, s_len, 2 * w), lambda b, h, i: (b, 0, h)),
                  pl.BlockSpec((1, s_len, HEAD), lambda b, h, i: (b, 0, 0)),
                  tab, tab, tab],
        out_specs=pl.BlockSpec((1, tq, w), lambda b, h, i: (b, i, h)),
        out_shape=jax.ShapeDtypeStruct((bsz, t, C_HEADS * HEAD), BF16),
        scratch_shapes=_softmax_scratch(heads, tq, HEAD),
        compiler_params=_params(3),
        name="mla_attention",
    )(q, q, kv, kr, *q_tables)


def _softmax_rows(scores):
    probs, sums = [], []
    for s in scores:
        p = jnp.exp2(s - jnp.max(s, axis=-1, keepdims=True))
        sums.append(jnp.sum(p, axis=-1, keepdims=True))
        probs.append(p.astype(BF16))
    return probs, sums


def _mem_attn_kernel(q_ref, k_ref, v_ref, o_ref):
    col = lambda g: slice(g * HEAD, (g + 1) * HEAD)
    scores = [_nt_dot((q_ref[0, :, col(g)].astype(F32) * (HEAD ** -0.5 * LOG2E)).astype(BF16),
                      k_ref[0, :, col(g)].astype(BF16)) for g in range(MEM_HEADS)]
    probs, sums = _softmax_rows(scores)
    for g in range(MEM_HEADS):
        o = jnp.dot(probs[g], v_ref[0, :, col(g)].astype(BF16), preferred_element_type=F32)
        o_ref[0, :, col(g)] = (o / sums[g]).astype(o_ref.dtype)


def memory_attention(q_src, q_col0, mem_k, mem_v, k_col0, v_col0, *, tq):
    bsz, t, _ = q_src.shape
    n_mem = mem_k.shape[1]
    assert q_col0 % MEM_WIDTH == 0 and k_col0 % MEM_WIDTH == 0 and v_col0 % MEM_WIDTH == 0
    qc, kc, vc = q_col0 // MEM_WIDTH, k_col0 // MEM_WIDTH, v_col0 // MEM_WIDTH
    return pl.pallas_call(
        _mem_attn_kernel,
        grid=(bsz, t // tq),
        in_specs=[pl.BlockSpec((1, tq, MEM_WIDTH), lambda b, i: (b, i, qc)),
                  pl.BlockSpec((1, n_mem, MEM_WIDTH), lambda b, i: (b, 0, kc)),
                  pl.BlockSpec((1, n_mem, MEM_WIDTH), lambda b, i: (b, 0, vc))],
        out_specs=pl.BlockSpec((1, tq, MEM_WIDTH), lambda b, i: (b, i, 0)),
        out_shape=jax.ShapeDtypeStruct((bsz, t, MEM_WIDTH), BF16),
        compiler_params=_params(2),
        name="memory_attention",
    )(q_src, mem_k, mem_v)


def _band_attn_kernel(q_ref, k_ref, v_ref, e_ref, o_ref, bias_ref, *, front_pad, heads):
    i = pl.program_id(2)

    @pl.when(i == 0)
    def _():
        for g in range(heads):
            e = jnp.broadcast_to(e_ref[g] * LOG2E, (BAND_TQ, BAND_TABLE))
            bias_ref[g] = pltpu.roll(e, 0, 1, stride=1, stride_axis=0)[:, :BAND_KEYS]

    rows = pl.ds(pl.multiple_of(i * BAND_TQ, BAND_TQ), BAND_KEYS)
    qq = lax.broadcasted_iota(jnp.int32, (BAND_TQ, BAND_KEYS), 0)
    kk = lax.broadcasted_iota(jnp.int32, (BAND_TQ, BAND_KEYS), 1)
    lo = jnp.maximum(front_pad - i * BAND_TQ, jnp.where(qq < CHUNK, 0, CHUNK))
    hi = jnp.where(qq < CHUNK, BAND_KEYS - CHUNK, BAND_KEYS)
    mask = (kk >= lo) & (kk < hi)
    col = lambda g: slice(g * HEAD, (g + 1) * HEAD)
    scores = [_nt_dot((q_ref[0, :, col(g)].astype(F32) * (HEAD ** -0.5 * LOG2E)).astype(BF16),
                      k_ref[0, rows, col(g)]) for g in range(heads)]
    scores = [jnp.where(mask, s + bias_ref[g], NEG_INF) for g, s in enumerate(scores)]
    probs, sums = _softmax_rows(scores)
    for g in range(heads):
        o = jnp.dot(probs[g], v_ref[0, rows, col(g)], preferred_element_type=F32)
        o_ref[0, :, col(g)] = (o / sums[g]).astype(o_ref.dtype)


def band_attention(q_src, k_pad, v_pad, bias_tab, *, front_pad, heads):
    bsz, t, _ = q_src.shape
    s_len = k_pad.shape[1]
    assert t % BAND_TQ == 0 and s_len == t + B_WINDOW and B_HEADS % heads == 0
    w = heads * HEAD
    kern = functools.partial(_band_attn_kernel, front_pad=front_pad, heads=heads)
    return pl.pallas_call(
        kern,
        grid=(bsz, B_HEADS // heads, t // BAND_TQ),
        in_specs=[pl.BlockSpec((1, BAND_TQ, w), lambda b, h, i: (b, i, h)),
                  pl.BlockSpec((1, s_len, w), lambda b, h, i: (b, 0, h)),
                  pl.BlockSpec((1, s_len, w), lambda b, h, i: (b, 0, h)),
                  pl.BlockSpec((heads, 1, BAND_TABLE), lambda b, h, i: (h, 0, 0))],
        out_specs=pl.BlockSpec((1, BAND_TQ, w), lambda b, h, i: (b, i, h)),
        out_shape=jax.ShapeDtypeStruct((bsz, t, B_WIDTH), BF16),
        scratch_shapes=[pltpu.VMEM((heads, BAND_TQ, BAND_KEYS), F32)],
        compiler_params=_params(3),
        name="band_attention",
    )(q_src, k_pad, v_pad, bias_tab)


def _band_bias_table(rel_bias):
    c = jnp.arange(BAND_TABLE)
    d = jnp.where(c <= BAND_KEYS, c, c - BAND_TABLE)
    idx = jnp.clip(B_WINDOW - d, -B_REL_CLIP, B_REL_CLIP) + B_REL_CLIP
    return rel_bias[:, idx][:, None, :]


def _stick_attn_kernel(q_ref, k_ref, v_ref, o_ref, tri_ref, *state_refs, tq, q_off, heads):
    acc_refs, run_refs = state_refs[:heads], state_refs[heads:]
    i = pl.program_id(2)
    q_start = q_off + i * tq
    tk = ATT_BLK

    @pl.when((pl.program_id(0) == 0) & (pl.program_id(1) == 0) & (i == 0))
    def _():
        r = lax.broadcasted_iota(jnp.int32, (2 * tk, tk), 0)
        c = lax.broadcasted_iota(jnp.int32, (2 * tk, tk), 1)
        tri_ref[...] = jnp.where(jnp.where(r >= tk, r - tk, r) > c, 1.0, 0.0).astype(BF16)

    qs = [(q_ref[0, :, g * HEAD:(g + 1) * HEAD].astype(F32) * (HEAD ** -0.5 * LOG2E)).astype(BF16)
          for g in range(heads)]
    for ref in state_refs:
        ref[...] = jnp.zeros_like(ref)

    def block(start, masked):
        rows = pl.ds(start, tk)
        if masked:
            qpos = q_start + lax.broadcasted_iota(jnp.int32, (tq, tk), 0)
            kpos = start + lax.broadcasted_iota(jnp.int32, (tq, tk), 1)
            allowed = kpos < qpos
        tri = tri_ref[...]
        col = lambda g: slice(g * HEAD, (g + 1) * HEAD)
        zs = [_nt_dot(qs[g], k_ref[0, rows, col(g)]) for g in range(heads)]
        log_betas, log_1m_betas = [], []
        for z in zs:
            log_beta = jnp.minimum(z, 0.0) - jnp.log2(1.0 + jnp.exp2(-jnp.abs(z)))
            log_1m_beta = log_beta - z
            log_betas.append(log_beta)
            log_1m_betas.append(jnp.where(allowed, log_1m_beta, 0.0) if masked else log_1m_beta)
        tails = []
        for g, log_1m_beta in enumerate(log_1m_betas):
            hi = log_1m_beta.astype(BF16)
            lo = (log_1m_beta - hi.astype(F32)).astype(BF16)
            tails.append(jnp.dot(jnp.concatenate([hi, lo], axis=1), tri, preferred_element_type=F32)
                         + run_refs[g][...])
        for g in range(heads):
            a = jnp.exp2(log_betas[g] + tails[g])
            if masked:
                a = jnp.where(allowed, a, 0.0)
            acc_refs[g][...] += jnp.dot(a.astype(BF16), v_ref[0, rows, col(g)], preferred_element_type=F32)
            run_refs[g][...] += jnp.sum(log_1m_betas[g], axis=-1, keepdims=True)

    block(pl.multiple_of(q_start, tk), True)
    n_before = q_start // tk

    def body(step, carry):
        block(pl.multiple_of((n_before - 1 - step) * tk, tk), False)
        return carry

    lax.fori_loop(0, n_before, body, 0)
    for g in range(heads):
        o_ref[0, :, g * HEAD:(g + 1) * HEAD] = acc_refs[g][...].astype(o_ref.dtype)


def stick_attention(src, q_col0, k_rows, k_col0, v_rows, v_col0, *, tq, q_off, heads):
    bsz, t, _ = src.shape
    s_len = k_rows.shape[1]
    _check_tiling(t, tq, q_off, s_len)
    w = heads * HEAD
    assert D_HEADS % heads == 0 and q_col0 % w == 0 and k_col0 % w == 0 and v_col0 % w == 0
    qc, kc, vc = q_col0 // w, k_col0 // w, v_col0 // w
    kern = functools.partial(_stick_attn_kernel, tq=tq, q_off=q_off, heads=heads)
    return pl.pallas_call(
        kern,
        grid=(bsz, D_HEADS // heads, t // tq),
        in_specs=[pl.BlockSpec((1, tq, w), lambda b, h, i: (b, i, qc + h)),
                  pl.BlockSpec((1, s_len, w), lambda b, h, i: (b, 0, kc + h)),
                  pl.BlockSpec((1, s_len, w), lambda b, h, i: (b, 0, vc + h))],
        out_specs=pl.BlockSpec((1, tq, w), lambda b, h, i: (b, i, h)),
        out_shape=jax.ShapeDtypeStruct((bsz, t, D_WIDTH), BF16),
        scratch_shapes=([pltpu.VMEM((2 * ATT_BLK, ATT_BLK), BF16)] + [pltpu.VMEM((tq, HEAD), F32)] * heads
                        + [pltpu.VMEM((tq, 1), F32)] * heads),
        compiler_params=_params(3),
        name="stick_attention",
    )(src, k_rows, v_rows)


def _pad_rows(a, front, back):
    return jnp.pad(a, ((0, 0), (front, back), (0, 0)))


def _round_up(n, mult):
    return (n + mult - 1) // mult * mult


def _with_past(past, new):
    bsz, p_len = past.shape[0], past.shape[1]
    rows = jnp.concatenate([past.reshape(bsz, p_len, new.shape[2]).astype(BF16), new], axis=1)
    return _pad_rows(rows, 0, _round_up(rows.shape[1], ATT_BLK) - rows.shape[1])


def _mixer_diff(kf, v_rows, zb, pos, past, params, lam_init):
    lq1, lk1, lq2, lk2, norm_g = params
    bsz, t, _ = kf.shape
    tables = _rope_tables(pos, A_ROT_DIM // 2, ROPE_THETA)
    k_rows, k_rows_b = rope_heads(kf, 0, 2 * A_HEADS, tables, A_ROT_DIM // 2, tt=min(t, 512))
    v_col0 = 2 * A_QK_WIDTH
    lam_vecs = jnp.stack([lq1, lq2, lk1, lk2]).astype(F32)
    g = norm_g.reshape(1, 2 * HEAD).astype(F32)
    if past is None:
        o = diff_attention(zb, k_rows_b, zb, v_col0, tables, lam_vecs, g, tq=ATT_TQ, q_off=0,
                           lam_init=lam_init, heads=2)
    else:
        k_all = _with_past(past[0], k_rows_b)
        v_all = _with_past(past[1], zb[:, :, v_col0:v_col0 + MIX_WIDTH])
        o = diff_attention(zb, k_all, v_all, 0, tables, lam_vecs, g, tq=t, q_off=past[0].shape[1],
                           lam_init=lam_init, heads=3)
    shape = (bsz, t, A_HEADS, 2 * HEAD)
    return o, (k_rows.reshape(shape), v_rows.reshape(shape))


def _mixer_band(k, v, zb, pos, past, params):
    (rel_bias,) = params
    bsz, t, _ = k.shape
    kb = zb[:, :, B_WIDTH:2 * B_WIDTH]
    vb = zb[:, :, 2 * B_WIDTH:3 * B_WIDTH]
    bias_tab = _band_bias_table(rel_bias.astype(F32))
    shape = (bsz, -1, B_HEADS, HEAD)
    if past is None:
        o = band_attention(zb, _pad_rows(kb, B_WINDOW, 0), _pad_rows(vb, B_WINDOW, 0), bias_tab,
                           front_pad=B_WINDOW, heads=4)
        keep = min(B_WINDOW, t)
        state = (k[:, t - keep:].reshape(shape), v[:, t - keep:].reshape(shape))
    else:
        buf_len = past[0].shape[1]
        assert t == CHUNK and buf_len == B_WINDOW
        k_all = jnp.concatenate([past[0].reshape(bsz, buf_len, B_WIDTH), k], axis=1)
        v_all = jnp.concatenate([past[1].reshape(bsz, buf_len, B_WIDTH), v], axis=1)
        q_pad = _pad_rows(zb[:, :, :B_WIDTH], BAND_TQ - t, 0)
        o = band_attention(q_pad, _pad_rows(k_all.astype(BF16), CHUNK, 0), _pad_rows(v_all.astype(BF16), CHUNK, 0),
                           bias_tab, front_pad=CHUNK, heads=6)[:, BAND_TQ - t:]
        state = (k_all[:, t:].reshape(shape), v_all[:, t:].reshape(shape))
    return o, state


def _mixer_mla(z, pos, past, params, w_uq, w_ukv):
    q_norm_g, kv_norm_g = params
    bsz, t, _ = z.shape
    half = C_ROPE_DIM // 2
    tables = _rope_tables(pos, half, C_ROPE_THETA)
    cq, latent, kr = mla_prep(z.reshape(bsz * t, z.shape[-1]), q_norm_g.reshape(1, -1).astype(F32),
                              kv_norm_g.reshape(1, -1).astype(F32), tables, seq_len=t, tt=min(t, 512))
    q = matmul(cq, w_uq, tm=min(bsz * t, 1024), tn=1024, out_dtypes=(BF16,)).reshape(bsz, t, -1)
    latent = latent.reshape(bsz, t, C_KV_RANK)
    kr = kr.reshape(bsz, t, HEAD)
    if past is None:
        lat_all, kr_all, q_off, tq, heads = latent, kr.astype(BF16), 0, ATT_TQ, 4
    else:
        q_off, tq, heads = past[0].shape[1], t, 6
        back = _round_up(q_off + t, ATT_BLK) - q_off - t
        lat_all = _pad_rows(jnp.concatenate([past[0], latent], axis=1), 0, back)
        kr_past = jnp.pad(past[1], ((0, 0), (0, 0), (0, HEAD - C_ROPE_DIM)))
        kr_all = _pad_rows(jnp.concatenate([kr_past, kr], axis=1), 0, back).astype(BF16)
    s_len = lat_all.shape[1]
    kv = matmul(lat_all.reshape(bsz * s_len, C_KV_RANK), w_ukv, tm=_pick(bsz * s_len, (1024, 512, 256)),
                tn=1024, out_dtypes=(BF16,)).reshape(bsz, s_len, -1)
    o = mla_attention(q, kv, kr_all, tables, tq=tq, q_off=q_off, heads=heads)
    return o, (latent, kr[:, :, :C_ROPE_DIM])


def _mixer_stick(k, v, zb, pos, past):
    bsz, t, _ = k.shape
    if past is None:
        o = stick_attention(zb, 0, zb, D_WIDTH, zb, 2 * D_WIDTH, tq=ATT_TQ, q_off=0, heads=4)
    else:
        k_all = _with_past(past[0], zb[:, :, D_WIDTH:2 * D_WIDTH])
        v_all = _with_past(past[1], zb[:, :, 2 * D_WIDTH:3 * D_WIDTH])
        o = stick_attention(zb, 0, k_all, 0, v_all, 0, tq=t, q_off=past[0].shape[1], heads=6)
    shape = (bsz, t, D_HEADS, HEAD)
    return o, (k.reshape(shape), v.reshape(shape))


def _trunk_layer(layer, x, pos, past, mem_kv, conv_state, mix_params, w):
    bsz, t, _ = x.shape
    m = bsz * t
    mixer = layer % N_MIXERS
    x2d = x.reshape(m, D_MODEL)
    z, zb = matmul(x2d, w["w_in"], tm=min(m, 1024), tn=1024, out_dtypes=(F32, BF16))
    z = z.reshape(bsz, t, -1)
    zb = zb.reshape(bsz, t, -1)
    zf = [z] if mixer == 2 else [z[:, :, MIX_WIDTH:2 * MIX_WIDTH], z[:, :, 2 * MIX_WIDTH:3 * MIX_WIDTH]]
    if mixer == 0:
        o_mix, state = _mixer_diff(*zf, zb, pos, past, mix_params, 0.8 - 0.6 * math.exp(-0.3 * layer))
    elif mixer == 1:
        o_mix, state = _mixer_band(*zf, zb, pos, past, mix_params)
    elif mixer == 2:
        o_mix, state = _mixer_mla(*zf, pos, past, mix_params, w["w_uq"], w["w_ukv"])
    else:
        o_mix, state = _mixer_stick(*zf, zb, pos, past)
    q_mem_col0 = C_MEM_COL if mixer == 2 else 3 * MIX_WIDTH
    o_mem = memory_attention(zb, q_mem_col0, mem_kv[0], mem_kv[1], mem_kv[2], mem_kv[3], tq=min(t, 512))
    x1 = proj_ln([o_mix.reshape(m, MIX_WIDTH), o_mem.reshape(m, MEM_WIDTH)], w["w_o"], x2d,
                 w["ln1_g"], w["ln1_b"], tm=512, name="out_proj_ln")
    g = ffn_up(x1, w["w_up"], w["conv_w"], w["conv_b"], conv_state, seq_len=t, tm=512, tn=512)
    x2 = proj_ln([g], w["w_down"], x1, w["ln2_g"], w["ln2_b"], tm=256, name="down_proj_ln")
    return x2.reshape(bsz, t, D_MODEL), state, x1.reshape(bsz, t, D_MODEL)


def _reorder_w_in_c(w):
    a, b = C_KR_COL, C_KR_COL + C_ROPE_DIM
    pad = jnp.zeros((w.shape[0], C_MEM_COL - b), w.dtype)
    return jnp.concatenate([w[:, :b], pad, w[:, b:]], axis=1)


def _reorder_w_uq(w):
    w = w.reshape(C_Q_RANK, C_HEADS, C_NOPE_DIM + C_ROPE_DIM)
    nope = w[:, :, :C_NOPE_DIM].reshape(C_Q_RANK, C_HEADS * HEAD)
    rope = jnp.pad(w[:, :, C_NOPE_DIM:], ((0, 0), (0, 0), (0, HEAD - C_ROPE_DIM)))
    return jnp.concatenate([nope, rope.reshape(C_Q_RANK, C_HEADS * HEAD)], axis=1)


def kernel(x_prompt, x_sample, mem_prompt, cache_a_k, cache_a_v, cache_b_k, cache_b_v, cache_c_latent, cache_c_krope, cache_d_k, cache_d_v, cache_mem_k, cache_mem_v, state_ffn_conv, w_in_a, w_in_b, w_in_c, w_in_d, diff_lambda_q1, diff_lambda_k1, diff_lambda_q2, diff_lambda_k2, diff_norm_g, band_rel_bias, mla_q_norm_g, mla_kv_norm_g, mla_w_uq, mla_w_ukv, w_mem_kv, w_o, ln1_g, ln1_b, w_up, conv_ffn_w, conv_ffn_b, w_down, ln2_g, ln2_b):
    n_p, t_p, _ = x_prompt.shape
    n_s, t_s, _ = x_sample.shape
    past_len = cache_d_k.shape[2]
    pos_p = jnp.arange(t_p)
    pos_s = past_len + jnp.arange(t_s)
    caches_by_type = ((cache_a_k, cache_a_v), (cache_b_k, cache_b_v),
                      (cache_c_latent, cache_c_krope), (cache_d_k, cache_d_v))
    params_by_type = ((diff_lambda_q1, diff_lambda_k1, diff_lambda_q2, diff_lambda_k2, diff_norm_g),
                      (band_rel_bias,), (mla_q_norm_g, mla_kv_norm_g), ())
    states_p = [([], []) for _ in range(N_MIXERS)]
    states_s = [([], []) for _ in range(N_MIXERS)]
    mem_k_p, mem_v_p, conv_p, conv_s = [], [], [], []
    n_mem = mem_prompt.shape[1]
    mem2d = mem_prompt.reshape(n_p * n_mem, D_MODEL)
    zero_state = jnp.zeros((n_p, 2, 2 * D_FF), F32)
    x_p, x_s = x_prompt, x_sample
    w_o_b, w_up_b, w_down_b, w_mem_kv_b = (a.astype(BF16) for a in (w_o, w_up, w_down, w_mem_kv))
    w_in_bf = {0: w_in_a.astype(BF16), 1: w_in_b.astype(BF16), 3: w_in_d.astype(BF16)}
    for i in range(DEPTH):
        mixer, j = i % N_MIXERS, i // N_MIXERS
        w = {
            "w_in": (_reorder_w_in_c(w_in_c[j]).astype(BF16) if mixer == 2 else (w_in_bf[mixer], j)),
            "w_o": (w_o_b, i),
            "ln1_g": ln1_g[i].reshape(1, -1), "ln1_b": ln1_b[i].reshape(1, -1),
            "w_up": (w_up_b, i),
            "conv_w": conv_ffn_w[i], "conv_b": conv_ffn_b[i].reshape(1, -1),
            "w_down": (w_down_b, i),
            "ln2_g": ln2_g[i].reshape(1, -1), "ln2_b": ln2_b[i].reshape(1, -1),
        }
        if mixer == 2:
            w["w_uq"] = _reorder_w_uq(mla_w_uq[j]).astype(BF16)
            w["w_ukv"] = mla_w_ukv[j].astype(BF16)
        mix_params = tuple(p[j] for p in params_by_type[mixer])
        kv_mem, kv_mem_b = matmul(mem2d, (w_mem_kv_b, i), tm=n_p * n_mem, tn=512, out_dtypes=(F32, BF16))
        kv_mem = kv_mem.reshape(n_p, n_mem, 2 * MEM_WIDTH)
        kv_mem_b = kv_mem_b.reshape(n_p, n_mem, 2 * MEM_WIDTH)
        x_p, st_p, x1_p = _trunk_layer(i, x_p, pos_p, None, (kv_mem_b, kv_mem_b, 0, MEM_WIDTH), zero_state,
                                       mix_params, w)
        past = (caches_by_type[mixer][0][j], caches_by_type[mixer][1][j])
        mem_s = (cache_mem_k[i].reshape(n_s, n_mem, MEM_WIDTH), cache_mem_v[i].reshape(n_s, n_mem, MEM_WIDTH), 0, 0)
        x_s, st_s, x1_s = _trunk_layer(i, x_s, pos_s, past, mem_s, state_ffn_conv[i], mix_params, w)
        tails = jnp.concatenate([x1_p[:, t_p - 2:].reshape(2 * n_p, D_MODEL),
                                 x1_s[:, t_s - 2:].reshape(2 * n_s, D_MODEL)], axis=0)
        n_tail = tails.shape[0]
        tails = jnp.pad(tails, ((0, _round_up(n_tail, 16) - n_tail), (0, 0)))
        u_tail = matmul(tails, w["w_up"], tm=tails.shape[0], tn=1024)
        conv_p.append(u_tail[:2 * n_p].reshape(n_p, 2, 2 * D_FF))
        conv_s.append(u_tail[2 * n_p:n_tail].reshape(n_s, 2, 2 * D_FF))
        for a in range(2):
            states_p[mixer][a].append(st_p[a])
            states_s[mixer][a].append(st_s[a])
        mem_k_p.append(kv_mem[:, :, :MEM_WIDTH].reshape(n_p, n_mem, MEM_HEADS, HEAD))
        mem_v_p.append(kv_mem[:, :, MEM_WIDTH:].reshape(n_p, n_mem, MEM_HEADS, HEAD))
    outs = [x_p, x_s]
    for st in states_p:
        outs += [jnp.stack(st[0]), jnp.stack(st[1])]
    outs += [jnp.stack(mem_k_p), jnp.stack(mem_v_p), jnp.stack(conv_p)]
    for st in states_s:
        outs += [jnp.stack(st[0]), jnp.stack(st[1])]
    outs.append(jnp.stack(conv_s))
    return tuple(outs)
```

```python
import functools
import math

import jax
import jax.numpy as jnp
from jax import lax
from jax.experimental import pallas as pl
from jax.experimental.pallas import tpu as pltpu

F32 = jnp.float32
BF16 = jnp.bfloat16

D_MODEL = 2048
DEPTH = 4
CHUNK = 64
CHUNK_SHIFT = 6
N_MIXERS = 4
MIX_WIDTH = 1536
MEM_HEADS = 4
MEM_WIDTH = 512
HEAD = 128
ROPE_THETA = 500000.0

A_HEADS = 6
A_QK_WIDTH = 1536
A_ROT_DIM = 32

B_HEADS = 12
B_WIDTH = 1536
B_WINDOW = 512
B_REL_CLIP = 128
BAND_TQ = 128
BAND_KEYS = B_WINDOW + BAND_TQ
BAND_TABLE = 768

C_HEADS = 12
C_Q_RANK = 768
C_KV_RANK = 512
C_NOPE_DIM = 128
C_ROPE_DIM = 64
C_ROPE_THETA = 10000.0
C_KR_COL = C_Q_RANK + C_KV_RANK
C_MEM_COL = 1536
C_IN_PAD = C_MEM_COL + MEM_WIDTH

D_HEADS = 12
D_WIDTH = 1536

D_FF = 5632
FFN_HALO = 16

ATT_TQ = 256
ATT_BLK = 256
ATT_WIDE = 512

DEEPNORM_ALPHA = (2 * DEPTH) ** 0.25
NORM_EPS = 1e-5
NEG_INF = -1e30
LOG2E = 1.0 / math.log(2.0)
STICK_DEAD_BITS = -160.0

VMEM_LIMIT = 56 * 1024 * 1024


def _params(n_axes):
    return pltpu.CompilerParams(dimension_semantics=("arbitrary",) * n_axes,
                                vmem_limit_bytes=VMEM_LIMIT)


def _pick(n, cands):
    for c in cands:
        if n % c == 0:
            return c
    return n


def _mm_kernel(a_ref, b_ref, *rest):
    *o_refs, abf_ref = rest

    @pl.when(pl.program_id(1) == 0)
    def _():
        abf_ref[...] = a_ref[...].astype(BF16)

    r = jnp.dot(abf_ref[...], b_ref[...], preferred_element_type=F32)
    for o_ref in o_refs:
        o_ref[...] = r.astype(o_ref.dtype)


def _layer_weight(w):
    if isinstance(w, tuple):
        return w[0], (None,), (w[1],)
    return w, (), ()


def matmul(a, b, *, tm, tn, out_dtypes=(F32,)):
    m, k = a.shape
    b, b_blk, b_idx = _layer_weight(b)
    n = b.shape[-1]
    assert m % tm == 0 and n % tn == 0, (a.shape, b.shape, tm, tn)
    outs = pl.pallas_call(
        _mm_kernel,
        grid=(m // tm, n // tn),
        in_specs=[pl.BlockSpec((tm, k), lambda i, j: (i, 0)),
                  pl.BlockSpec(b_blk + (k, tn), lambda i, j: b_idx + (0, j))],
        out_specs=[pl.BlockSpec((tm, tn), lambda i, j: (i, j)) for _ in out_dtypes],
        out_shape=[jax.ShapeDtypeStruct((m, n), dt) for dt in out_dtypes],
        scratch_shapes=[pltpu.VMEM((tm, k), BF16)],
        compiler_params=_params(2),
        name="matmul",
    )(a, b)
    return outs[0] if len(out_dtypes) == 1 else outs


def _layer_norm_rows(y, g, b):
    mu = jnp.mean(y, axis=-1, keepdims=True)
    d = y - mu
    var = jnp.mean(d * d, axis=-1, keepdims=True)
    return d * lax.rsqrt(var + NORM_EPS) * g + b


def _proj_ln_kernel(*refs, n_parts):
    a_refs = refs[:n_parts]
    w_ref, x_ref, g_ref, b_ref, o_ref = refs[n_parts:]
    f, k0 = None, 0
    for a_ref in a_refs:
        k = a_ref.shape[1]
        part = jnp.dot(a_ref[...], w_ref[k0:k0 + k, :], preferred_element_type=F32)
        f = part if f is None else f + part
        k0 += k
    y = DEEPNORM_ALPHA * x_ref[...] + f
    o_ref[...] = _layer_norm_rows(y, g_ref[...], b_ref[...])


def proj_ln(a_parts, w, x, g, b, *, tm, name):
    m = x.shape[0]
    w, w_blk, w_idx = _layer_weight(w)
    k_total, n = w.shape[-2:]
    assert sum(a.shape[1] for a in a_parts) == k_total and m % tm == 0
    row = lambda i: (i, 0)
    fixed = lambda i: (0, 0)
    return pl.pallas_call(
        functools.partial(_proj_ln_kernel, n_parts=len(a_parts)),
        grid=(m // tm,),
        in_specs=([pl.BlockSpec((tm, a.shape[1]), row) for a in a_parts]
                  + [pl.BlockSpec(w_blk + (k_total, n), lambda i: w_idx + (0, 0), pipeline_mode=pl.Buffered(1)),
                     pl.BlockSpec((tm, n), row), pl.BlockSpec((1, n), fixed), pl.BlockSpec((1, n), fixed)]),
        out_specs=pl.BlockSpec((tm, n), row),
        out_shape=jax.ShapeDtypeStruct((m, n), F32),
        compiler_params=_params(1),
        name=name,
    )(*a_parts, w, x, g, b)


def _silu(x):
    return x * (1.0 / (1.0 + jnp.exp(-x)))


def _ffn_up_kernel(x_ref, xh_ref, wg_ref, wv_ref, cwg_ref, cwv_ref, cbg_ref, cbv_ref,
                   sg_ref, sv_ref, o_ref, xcat_ref, *, tiles_per_seq, seq_starts):
    i = pl.program_id(0)
    j = pl.program_id(1)

    @pl.when(j == 0)
    def _():
        xcat_ref[:FFN_HALO, :] = xh_ref[...].astype(BF16)
        xcat_ref[FFN_HALO:, :] = x_ref[...].astype(BF16)

    def conv(u, cw_ref, cb_ref):
        cw = cw_ref[...]
        p1 = pltpu.roll(u, 1, 0)[FFN_HALO:]
        p2 = pltpu.roll(u, 2, 0)[FFN_HALO:]
        return cb_ref[...] + cw[0:1] * p2 + cw[1:2] * p1 + cw[2:3] * u[FFN_HALO:]

    xe = xcat_ref[...]
    ug = jnp.dot(xe, wg_ref[...], preferred_element_type=F32)
    uv = jnp.dot(xe, wv_ref[...], preferred_element_type=F32)
    o_ref[...] = (_silu(conv(ug, cwg_ref, cbg_ref)) * conv(uv, cwv_ref, cbv_ref)).astype(o_ref.dtype)

    def fix_sequence_start(seq, row0):
        rid = lax.broadcasted_iota(jnp.int32, (FFN_HALO, o_ref.shape[1]), 0)

        def conv_head(u, cw_ref, cb_ref, st_ref):
            cw = cw_ref[...]
            cur = u[FFN_HALO + row0:2 * FFN_HALO + row0]
            st = st_ref[seq]
            s0, s1 = st[0:1], st[1:2]
            p1 = jnp.where(rid == 0, s1, pltpu.roll(cur, 1, 0))
            p2 = jnp.where(rid == 0, s0, jnp.where(rid == 1, s1, pltpu.roll(cur, 2, 0)))
            return cb_ref[...] + cw[0:1] * p2 + cw[1:2] * p1 + cw[2:3] * cur

        hg = conv_head(ug, cwg_ref, cbg_ref, sg_ref)
        hv = conv_head(uv, cwv_ref, cbv_ref, sv_ref)
        o_ref[row0:row0 + FFN_HALO, :] = (_silu(hg) * hv).astype(o_ref.dtype)

    if tiles_per_seq > 1:
        pl.when(i % tiles_per_seq == 0)(lambda: fix_sequence_start(0, 0))
    else:
        for seq, row0 in enumerate(seq_starts):
            fix_sequence_start(seq, row0)


def ffn_up(x, w_up, conv_w, conv_b, state, *, seq_len, tm, tn):
    m, k = x.shape
    w_up, w_blk, w_idx = _layer_weight(w_up)
    assert (seq_len % tm == 0 or tm % seq_len == 0) and seq_len % FFN_HALO == 0 and D_FF % tn == 0
    tiles_per_seq = max(seq_len // tm, 1)
    seqs_per_tile = max(tm // seq_len, 1)
    nf = D_FF // tn
    halo_blocks = tm // FFN_HALO
    kern = functools.partial(_ffn_up_kernel, tiles_per_seq=tiles_per_seq,
                             seq_starts=tuple(q * seq_len for q in range(seqs_per_tile)))
    seq_block = lambda i: i * seqs_per_tile // tiles_per_seq // seqs_per_tile
    return pl.pallas_call(
        kern,
        grid=(m // tm, nf),
        in_specs=[
            pl.BlockSpec((tm, k), lambda i, j: (i, 0)),
            pl.BlockSpec((FFN_HALO, k), lambda i, j: (jnp.maximum(i * halo_blocks - 1, 0), 0)),
            pl.BlockSpec(w_blk + (k, tn), lambda i, j: w_idx + (0, j)),
            pl.BlockSpec(w_blk + (k, tn), lambda i, j: w_idx + (0, j + nf)),
            pl.BlockSpec((3, tn), lambda i, j: (0, j)),
            pl.BlockSpec((3, tn), lambda i, j: (0, j + nf)),
            pl.BlockSpec((1, tn), lambda i, j: (0, j)),
            pl.BlockSpec((1, tn), lambda i, j: (0, j + nf)),
            pl.BlockSpec((seqs_per_tile, 2, tn), lambda i, j: (seq_block(i), 0, j)),
            pl.BlockSpec((seqs_per_tile, 2, tn), lambda i, j: (seq_block(i), 0, j + nf)),
        ],
        out_specs=pl.BlockSpec((tm, tn), lambda i, j: (i, j)),
        out_shape=jax.ShapeDtypeStruct((m, D_FF), BF16),
        scratch_shapes=[pltpu.VMEM((tm + FFN_HALO, k), BF16)],
        compiler_params=_params(2),
        name="ffn_up",
    )(x, x, w_up, w_up, conv_w, conv_w, conv_b, conv_b, state, state)


def _rope_tables(pos, half, theta):
    inv_freq = theta ** (-jnp.arange(half, dtype=F32) / half)
    ang = pos.astype(F32)[:, None] * inv_freq[None, :]
    cos, sin = jnp.cos(ang), jnp.sin(ang)
    t = pos.shape[0]
    rest = HEAD - 2 * half
    cos_t = jnp.concatenate([cos, cos, jnp.ones((t, rest), F32)], axis=1)
    sin_up = jnp.concatenate([jnp.zeros((t, half), F32), sin, jnp.zeros((t, rest), F32)], axis=1)
    sin_dn = jnp.concatenate([-sin, jnp.zeros((t, half + rest), F32)], axis=1)
    return cos_t, sin_up, sin_dn


def _rope(x, cos_t, sin_up, sin_dn, half):
    return x * cos_t + pltpu.roll(x, half, 1) * sin_up + pltpu.roll(x, HEAD - half, 1) * sin_dn


def _rope_heads_kernel(x_ref, c_ref, su_ref, sd_ref, o_ref, ob_ref, *, half, n_heads):
    tabs = (c_ref[...], su_ref[...], sd_ref[...])
    for h in range(n_heads):
        cols = slice(h * HEAD, (h + 1) * HEAD)
        r = _rope(x_ref[0, :, cols], *tabs, half)
        o_ref[0, :, cols] = r
        ob_ref[0, :, cols] = r.astype(BF16)


def rope_heads(z, col0, n_heads, tables, half, *, tt):
    bsz, t, _ = z.shape
    w = n_heads * HEAD
    assert col0 % w == 0
    c0 = col0 // w
    tab = pl.BlockSpec((tt, HEAD), lambda b, i: (i, 0))
    out = pl.BlockSpec((1, tt, w), lambda b, i: (b, i, 0))
    return pl.pallas_call(
        functools.partial(_rope_heads_kernel, half=half, n_heads=n_heads),
        grid=(bsz, t // tt),
        in_specs=[pl.BlockSpec((1, tt, w), lambda b, i: (b, i, c0)), tab, tab, tab],
        out_specs=[out, out],
        out_shape=[jax.ShapeDtypeStruct((bsz, t, w), F32), jax.ShapeDtypeStruct((bsz, t, w), BF16)],
        compiler_params=_params(2),
        name="rope_heads",
    )(z, *tables)


def _rms_rows(x, g):
    return x * lax.rsqrt(jnp.mean(x * x, axis=-1, keepdims=True) + NORM_EPS) * g


def _mla_prep_kernel(z_ref, gq_ref, gkv_ref, c_ref, su_ref, sd_ref, cq_ref, lat_ref, kr_ref):
    z = z_ref[...]
    cq_ref[...] = _rms_rows(z[:, :C_Q_RANK], gq_ref[...]).astype(cq_ref.dtype)
    lat_ref[...] = _rms_rows(z[:, C_Q_RANK:C_KR_COL], gkv_ref[...])
    kr = z[:, C_KR_COL:C_KR_COL + HEAD]
    kr_ref[...] = _rope(kr, c_ref[...], su_ref[...], sd_ref[...], C_ROPE_DIM // 2)


def mla_prep(z2d, gq, gkv, tables, *, seq_len, tt):
    m = z2d.shape[0]
    nt = seq_len // tt
    row = lambda i: (i, 0)
    fixed = lambda i: (0, 0)
    tab = pl.BlockSpec((tt, HEAD), lambda i: (i % nt, 0))
    return pl.pallas_call(
        _mla_prep_kernel,
        grid=(m // tt,),
        in_specs=[pl.BlockSpec((tt, C_MEM_COL), row), pl.BlockSpec((1, C_Q_RANK), fixed),
                  pl.BlockSpec((1, C_KV_RANK), fixed), tab, tab, tab],
        out_specs=[pl.BlockSpec((tt, C_Q_RANK), row), pl.BlockSpec((tt, C_KV_RANK), row),
                   pl.BlockSpec((tt, HEAD), row)],
        out_shape=[jax.ShapeDtypeStruct((m, C_Q_RANK), BF16),
                   jax.ShapeDtypeStruct((m, C_KV_RANK), F32),
                   jax.ShapeDtypeStruct((m, HEAD), F32)],
        compiler_params=_params(1),
        name="mla_prep",
    )(z2d, gq, gkv, *tables)


def _nt_dot(a, b):
    return lax.dot_general(a, b, (((1,), (1,)), ((), ())), preferred_element_type=F32)


def _lanes(x, n):
    return x if n == HEAD else jnp.concatenate([x] * (n // HEAD), axis=1)


def _chunk_mask(tq, tk, q_start, k_start):
    qpos = q_start + lax.broadcasted_iota(jnp.int32, (tq, tk), 0)
    kpos = k_start + lax.broadcasted_iota(jnp.int32, (tq, tk), 1)
    return jnp.right_shift(kpos, CHUNK_SHIFT) <= jnp.right_shift(qpos, CHUNK_SHIFT)


def _softmax_steps(scores, values, stats):
    probs, alphas = [], []
    for s, (m_ref, l_ref, _) in zip(scores, stats):
        tk = s.shape[1]
        m_old = m_ref[...]
        m_new = jnp.maximum(m_old, jnp.max(s, axis=-1, keepdims=True))
        alpha = jnp.exp2(m_old - m_new)
        p = jnp.exp2(s - _lanes(m_new, tk))
        psum = p[:, :HEAD]
        for c in range(1, tk // HEAD):
            psum = psum + p[:, c * HEAD:(c + 1) * HEAD]
        l_ref[...] = alpha * l_ref[...] + psum
        m_ref[...] = m_new
        probs.append(p.astype(BF16))
        alphas.append(alpha)
    for p, v, alpha, (_, _, acc_ref) in zip(probs, values, alphas, stats):
        acc_ref[...] = (_lanes(alpha, acc_ref.shape[-1]) * acc_ref[...]
                        + jnp.dot(p, v, preferred_element_type=F32))


def _softmax_scratch(n_chains, tq, dv):
    return [pltpu.VMEM((tq, HEAD), F32), pltpu.VMEM((tq, HEAD), F32), pltpu.VMEM((tq, dv), F32)] * n_chains


def _softmax_stats(scratch_refs):
    stats = [tuple(scratch_refs[3 * c:3 * c + 3]) for c in range(len(scratch_refs) // 3)]
    for m_ref, l_ref, acc_ref in stats:
        m_ref[...] = jnp.full_like(m_ref, NEG_INF)
        l_ref[...] = jnp.zeros_like(l_ref)
        acc_ref[...] = jnp.zeros_like(acc_ref)
    return stats


def _softmax_result(stat):
    _, l_ref, acc_ref = stat
    return acc_ref[...] / jnp.sum(l_ref[...], axis=-1, keepdims=True)


def _causal_key_blocks(block, q_start):
    n_wide = q_start // ATT_WIDE

    def body(j, carry):
        block(pl.multiple_of(j * ATT_WIDE, ATT_WIDE), ATT_WIDE, False)
        return carry

    lax.fori_loop(0, n_wide, body, 0)
    rest = n_wide * ATT_WIDE

    @pl.when(q_start - rest >= ATT_BLK)
    def _():
        block(pl.multiple_of(rest, ATT_BLK), ATT_BLK, False)

    block(pl.multiple_of(q_start, ATT_BLK), ATT_BLK, True)


def _check_tiling(t, tq, q_off, s_len):
    assert t % tq == 0 and q_off % ATT_BLK == 0 and s_len % ATT_BLK == 0
    assert tq == ATT_BLK or (t == tq and tq <= ATT_BLK), "own chunks must sit in one ATT_BLK key block"
    assert q_off + t <= s_len


def _diff_attn_kernel(q_ref, k_ref, v_ref, c_ref, su_ref, sd_ref, lam_ref, g_ref, o_ref,
                      *scratch_refs, tq, q_off, lam_init, heads):
    i = pl.program_id(2)
    q_start = q_off + i * tq
    scale = HEAD ** -0.5 * LOG2E
    tabs = (c_ref[...], su_ref[...], sd_ref[...])
    n_slots = 2 * heads
    qs = [(_rope(q_ref[0, :, sl * HEAD:(sl + 1) * HEAD].astype(F32), *tabs, A_ROT_DIM // 2) * scale).astype(BF16)
          for sl in range(n_slots)]
    stats = _softmax_stats(scratch_refs)

    def block(start, width, masked):
        rows = pl.ds(start, width)
        scores = [_nt_dot(qs[sl], k_ref[0, rows, sl * HEAD:(sl + 1) * HEAD]) for sl in range(n_slots)]
        if masked:
            mask = _chunk_mask(tq, width, q_start, start)
            scores = [jnp.where(mask, s, NEG_INF) for s in scores]
        values = [v_ref[0, rows, (sl // 2) * 2 * HEAD:(sl // 2 + 1) * 2 * HEAD] for sl in range(n_slots)]
        _softmax_steps(scores, values, stats)

    _causal_key_blocks(block, q_start)

    lam_v = lam_ref[...]
    dots = jnp.sum(lam_v[0:2] * lam_v[2:4], axis=-1, keepdims=True)
    lam = jnp.exp(dots[0:1]) - jnp.exp(dots[1:2]) + lam_init
    for g in range(heads):
        o = _softmax_result(stats[2 * g]) - lam * _softmax_result(stats[2 * g + 1])
        o_ref[0, :, g * 2 * HEAD:(g + 1) * 2 * HEAD] = (
            _rms_rows(o, g_ref[...]) * (1.0 - lam_init)).astype(o_ref.dtype)


def diff_attention(q_src, k_rows, v_src, v_col0, q_tables, lam_vecs, norm_g, *, tq, q_off, lam_init, heads):
    bsz, t, _ = q_src.shape
    s_len = k_rows.shape[1]
    _check_tiling(t, tq, q_off, s_len)
    w = heads * 2 * HEAD
    assert A_HEADS % heads == 0 and v_col0 % w == 0
    vc = v_col0 // w
    tab = pl.BlockSpec((tq, HEAD), lambda b, h, i: (i, 0))
    kern = functools.partial(_diff_attn_kernel, tq=tq, q_off=q_off, lam_init=lam_init, heads=heads)
    return pl.pallas_call(
        kern,
        grid=(bsz, A_HEADS // heads, t // tq),
        in_specs=[pl.BlockSpec((1, tq, w), lambda b, h, i: (b, i, h)),
                  pl.BlockSpec((1, s_len, w), lambda b, h, i: (b, 0, h)),
                  pl.BlockSpec((1, s_len, w), lambda b, h, i: (b, 0, vc + h)),
                  tab, tab, tab,
                  pl.BlockSpec((4, HEAD), lambda b, h, i: (0, 0)),
                  pl.BlockSpec((1, 2 * HEAD), lambda b, h, i: (0, 0))],
        out_specs=pl.BlockSpec((1, tq, w), lambda b, h, i: (b, i, h)),
        out_shape=jax.ShapeDtypeStruct((bsz, t, A_HEADS * 2 * HEAD), BF16),
        scratch_shapes=_softmax_scratch(2 * heads, tq, 2 * HEAD),
        compiler_params=_params(3),
        name="diff_attention",
    )(q_src, k_rows, v_src, *q_tables, lam_vecs, norm_g)


def _mla_attn_kernel(qn_ref, qr_ref, kv_ref, kr_ref, c_ref, su_ref, sd_ref, o_ref,
                     *scratch_refs, tq, q_off, heads):
    i = pl.program_id(2)
    q_start = q_off + i * tq
    scale = (C_NOPE_DIM + C_ROPE_DIM) ** -0.5 * LOG2E
    tabs = (c_ref[...], su_ref[...], sd_ref[...])
    qn = [(qn_ref[0, :, g * HEAD:(g + 1) * HEAD].astype(F32) * scale).astype(BF16) for g in range(heads)]
    qr = [(_rope(qr_ref[0, :, g * HEAD:(g + 1) * HEAD].astype(F32), *tabs, C_ROPE_DIM // 2) * scale).astype(BF16)
          for g in range(heads)]
    qs = [jnp.concatenate([qn[g], qr[g]], axis=1) for g in range(heads)]
    stats = _softmax_stats(scratch_refs)

    def block(start, width, masked):
        rows = pl.ds(start, width)
        kr = kr_ref[0, rows, :]
        scores = [_nt_dot(qs[g], jnp.concatenate([kv_ref[0, rows, 2 * g * HEAD:(2 * g + 1) * HEAD], kr], axis=1))
                  for g in range(heads)]
        if masked:
            mask = _chunk_mask(tq, width, q_start, start)
            scores = [jnp.where(mask, s, NEG_INF) for s in scores]
        values = [kv_ref[0, rows, (2 * g + 1) * HEAD:(2 * g + 2) * HEAD] for g in range(heads)]
        _softmax_steps(scores, values, stats)

    _causal_key_blocks(block, q_start)
    for g in range(heads):
        o_ref[0, :, g * HEAD:(g + 1) * HEAD] = _softmax_result(stats[g]).astype(o_ref.dtype)


def mla_attention(q, kv, kr, q_tables, *, tq, q_off, heads):
    bsz, t, _ = q.shape
    s_len = kv.shape[1]
    _check_tiling(t, tq, q_off, s_len)
    assert C_HEADS % heads == 0
    w = heads * HEAD
    n_groups = C_HEADS // heads
    tab = pl.BlockSpec((tq, HEAD), lambda b, h, i: (i, 0))
    kern = functools.partial(_mla_attn_kernel, tq=tq, q_off=q_off, heads=heads)
    return pl.pallas_call(
        kern,
        grid=(bsz, n_groups, t // tq),
        in_specs=[pl.BlockSpec((1, tq, w), lambda b, h, i: (b, i, h)),
                  pl.BlockSpec((1, tq, w), lambda b, h, i: (b, i, n_groups + h)),
                  pl.BlockSpec((1, s_len, 2 * w), lambda b, h, i: (b, 0, h)),
                  pl.BlockSpec((1, s_len, HEAD), lambda b, h, i: (b, 0, 0)),
                  tab, tab, tab],
        out_specs=pl.BlockSpec((1, tq, w), lambda b, h, i: (b, i, h)),
        out_shape=jax.ShapeDtypeStruct((bsz, t, C_HEADS * HEAD), BF16),
        scratch_shapes=_softmax_scratch(heads, tq, HEAD),
        compiler_params=_params(3),
        name="mla_attention",
    )(q, q, kv, kr, *q_tables)


def _softmax_rows(scores):
    probs, sums = [], []
    for s in scores:
        p = jnp.exp2(s - jnp.max(s, axis=-1, keepdims=True))
        sums.append(jnp.sum(p, axis=-1, keepdims=True))
        probs.append(p.astype(BF16))
    return probs, sums


def _mem_attn_kernel(q_ref, k_ref, v_ref, o_ref):
    col = lambda g: slice(g * HEAD, (g + 1) * HEAD)
    scores = [_nt_dot((q_ref[0, :, col(g)].astype(F32) * (HEAD ** -0.5 * LOG2E)).astype(BF16),
                      k_ref[0, :, col(g)].astype(BF16)) for g in range(MEM_HEADS)]
    probs, sums = _softmax_rows(scores)
    for g in range(MEM_HEADS):
        o = jnp.dot(probs[g], v_ref[0, :, col(g)].astype(BF16), preferred_element_type=F32)
        o_ref[0, :, col(g)] = (o / sums[g]).astype(o_ref.dtype)


def memory_attention(q_src, q_col0, mem_k, mem_v, k_col0, v_col0, *, tq):
    bsz, t, _ = q_src.shape
    n_mem = mem_k.shape[1]
    assert q_col0 % MEM_WIDTH == 0 and k_col0 % MEM_WIDTH == 0 and v_col0 % MEM_WIDTH == 0
    qc, kc, vc = q_col0 // MEM_WIDTH, k_col0 // MEM_WIDTH, v_col0 // MEM_WIDTH
    return pl.pallas_call(
        _mem_attn_kernel,
        grid=(bsz, t // tq),
        in_specs=[pl.BlockSpec((1, tq, MEM_WIDTH), lambda b, i: (b, i, qc)),
                  pl.BlockSpec((1, n_mem, MEM_WIDTH), lambda b, i: (b, 0, kc)),
                  pl.BlockSpec((1, n_mem, MEM_WIDTH), lambda b, i: (b, 0, vc))],
        out_specs=pl.BlockSpec((1, tq, MEM_WIDTH), lambda b, i: (b, i, 0)),
        out_shape=jax.ShapeDtypeStruct((bsz, t, MEM_WIDTH), BF16),
        compiler_params=_params(2),
        name="memory_attention",
    )(q_src, mem_k, mem_v)


def _band_attn_kernel(q_ref, k_ref, v_ref, e_ref, o_ref, bias_ref, *, front_pad, heads):
    i = pl.program_id(2)

    @pl.when(i == 0)
    def _():
        for g in range(heads):
            e = jnp.broadcast_to(e_ref[g] * LOG2E, (BAND_TQ, BAND_TABLE))
            bias_ref[g] = pltpu.roll(e, 0, 1, stride=1, stride_axis=0)[:, :BAND_KEYS]

    rows = pl.ds(pl.multiple_of(i * BAND_TQ, BAND_TQ), BAND_KEYS)
    qq = lax.broadcasted_iota(jnp.int32, (BAND_TQ, BAND_KEYS), 0)
    kk = lax.broadcasted_iota(jnp.int32, (BAND_TQ, BAND_KEYS), 1)
    lo = jnp.maximum(front_pad - i * BAND_TQ, jnp.where(qq < CHUNK, 0, CHUNK))
    hi = jnp.where(qq < CHUNK, BAND_KEYS - CHUNK, BAND_KEYS)
    mask = (kk >= lo) & (kk < hi)
    col = lambda g: slice(g * HEAD, (g + 1) * HEAD)
    scores = [_nt_dot((q_ref[0, :, col(g)].astype(F32) * (HEAD ** -0.5 * LOG2E)).astype(BF16),
                      k_ref[0, rows, col(g)]) for g in range(heads)]
    scores = [jnp.where(mask, s + bias_ref[g], NEG_INF) for g, s in enumerate(scores)]
    probs, sums = _softmax_rows(scores)
    for g in range(heads):
        o = jnp.dot(probs[g], v_ref[0, rows, col(g)], preferred_element_type=F32)
        o_ref[0, :, col(g)] = (o / sums[g]).astype(o_ref.dtype)


def band_attention(q_src, k_pad, v_pad, bias_tab, *, front_pad, heads):
    bsz, t, _ = q_src.shape
    s_len = k_pad.shape[1]
    assert t % BAND_TQ == 0 and s_len == t + B_WINDOW and B_HEADS % heads == 0
    w = heads * HEAD
    kern = functools.partial(_band_attn_kernel, front_pad=front_pad, heads=heads)
    return pl.pallas_call(
        kern,
        grid=(bsz, B_HEADS // heads, t // BAND_TQ),
        in_specs=[pl.BlockSpec((1, BAND_TQ, w), lambda b, h, i: (b, i, h)),
                  pl.BlockSpec((1, s_len, w), lambda b, h, i: (b, 0, h)),
                  pl.BlockSpec((1, s_len, w), lambda b, h, i: (b, 0, h)),
                  pl.BlockSpec((heads, 1, BAND_TABLE), lambda b, h, i: (h, 0, 0))],
        out_specs=pl.BlockSpec((1, BAND_TQ, w), lambda b, h, i: (b, i, h)),
        out_shape=jax.ShapeDtypeStruct((bsz, t, B_WIDTH), BF16),
        scratch_shapes=[pltpu.VMEM((heads, BAND_TQ, BAND_KEYS), F32)],
        compiler_params=_params(3),
        name="band_attention",
    )(q_src, k_pad, v_pad, bias_tab)


def _band_bias_table(rel_bias):
    c = jnp.arange(BAND_TABLE)
    d = jnp.where(c <= BAND_KEYS, c, c - BAND_TABLE)
    idx = jnp.clip(B_WINDOW - d, -B_REL_CLIP, B_REL_CLIP) + B_REL_CLIP
    return rel_bias[:, idx][:, None, :]


def _stick_attn_kernel(q_ref, k_ref, v_ref, o_ref, tri_ref, *state_refs, tq, q_off, heads):
    acc_refs, run_refs = state_refs[:heads], state_refs[heads:]
    i = pl.program_id(2)
    q_start = q_off + i * tq
    tk = ATT_BLK

    @pl.when((pl.program_id(0) == 0) & (pl.program_id(1) == 0) & (i == 0))
    def _():
        r = lax.broadcasted_iota(jnp.int32, (2 * tk, tk), 0)
        c = lax.broadcasted_iota(jnp.int32, (2 * tk, tk), 1)
        tri_ref[...] = jnp.where(jnp.where(r >= tk, r - tk, r) > c, 1.0, 0.0).astype(BF16)

    qs = [(q_ref[0, :, g * HEAD:(g + 1) * HEAD].astype(F32) * (HEAD ** -0.5 * LOG2E)).astype(BF16)
          for g in range(heads)]
    for ref in state_refs:
        ref[...] = jnp.zeros_like(ref)

    def block(start, masked):
        rows = pl.ds(start, tk)
        if masked:
            qpos = q_start + lax.broadcasted_iota(jnp.int32, (tq, tk), 0)
            kpos = start + lax.broadcasted_iota(jnp.int32, (tq, tk), 1)
            allowed = kpos < qpos
        tri = tri_ref[...]
        col = lambda g: slice(g * HEAD, (g + 1) * HEAD)
        zs = [_nt_dot(qs[g], k_ref[0, rows, col(g)]) for g in range(heads)]
        log_betas, log_1m_betas = [], []
        for z in zs:
            log_beta = jnp.minimum(z, 0.0) - jnp.log2(1.0 + jnp.exp2(-jnp.abs(z)))
            log_1m_beta = log_beta - z
            log_betas.append(log_beta)
            log_1m_betas.append(jnp.where(allowed, log_1m_beta, 0.0) if masked else log_1m_beta)
        tails = []
        for g, log_1m_beta in enumerate(log_1m_betas):
            hi = log_1m_beta.astype(BF16)
            lo = (log_1m_beta - hi.astype(F32)).astype(BF16)
            tails.append(jnp.dot(jnp.concatenate([hi, lo], axis=1), tri, preferred_element_type=F32)
                         + run_refs[g][...])
        for g in range(heads):
            a = jnp.exp2(log_betas[g] + tails[g])
            if masked:
                a = jnp.where(allowed, a, 0.0)
            acc_refs[g][...] += jnp.dot(a.astype(BF16), v_ref[0, rows, col(g)], preferred_element_type=F32)
            run_refs[g][...] += jnp.sum(log_1m_betas[g], axis=-1, keepdims=True)

    block(pl.multiple_of(q_start, tk), True)
    n_before = q_start // tk

    def any_stick_left():
        top = run_refs[0][...]
        for g in range(1, heads):
            top = jnp.maximum(top, run_refs[g][...])
        return (jnp.max(top) > STICK_DEAD_BITS).astype(jnp.int32)

    def more(carry):
        step, live = carry
        return (step < n_before) & (live > 0)

    def body(carry):
        step, _ = carry
        block(pl.multiple_of((n_before - 1 - step) * tk, tk), False)
        return step + 1, any_stick_left()

    lax.while_loop(more, body, (jnp.int32(0), any_stick_left()))
    for g in range(heads):
        o_ref[0, :, g * HEAD:(g + 1) * HEAD] = acc_refs[g][...].astype(o_ref.dtype)


def stick_attention(src, q_col0, k_rows, k_col0, v_rows, v_col0, *, tq, q_off, heads):
    bsz, t, _ = src.shape
    s_len = k_rows.shape[1]
    _check_tiling(t, tq, q_off, s_len)
    w = heads * HEAD
    assert D_HEADS % heads == 0 and q_col0 % w == 0 and k_col0 % w == 0 and v_col0 % w == 0
    qc, kc, vc = q_col0 // w, k_col0 // w, v_col0 // w
    kern = functools.partial(_stick_attn_kernel, tq=tq, q_off=q_off, heads=heads)
    return pl.pallas_call(
        kern,
        grid=(bsz, D_HEADS // heads, t // tq),
        in_specs=[pl.BlockSpec((1, tq, w), lambda b, h, i: (b, i, qc + h)),
                  pl.BlockSpec((1, s_len, w), lambda b, h, i: (b, 0, kc + h)),
                  pl.BlockSpec((1, s_len, w), lambda b, h, i: (b, 0, vc + h))],
        out_specs=pl.BlockSpec((1, tq, w), lambda b, h, i: (b, i, h)),
        out_shape=jax.ShapeDtypeStruct((bsz, t, D_WIDTH), BF16),
        scratch_shapes=([pltpu.VMEM((2 * ATT_BLK, ATT_BLK), BF16)] + [pltpu.VMEM((tq, HEAD), F32)] * heads
                        + [pltpu.VMEM((tq, 1), F32)] * heads),
        compiler_params=_params(3),
        name="stick_attention",
    )(src, k_rows, v_rows)


def _pad_rows(a, front, back):
    return jnp.pad(a, ((0, 0), (front, back), (0, 0)))


def _round_up(n, mult):
    return (n + mult - 1) // mult * mult


def _with_past(past, new):
    bsz, p_len = past.shape[0], past.shape[1]
    rows = jnp.concatenate([past.reshape(bsz, p_len, new.shape[2]).astype(BF16), new], axis=1)
    return _pad_rows(rows, 0, _round_up(rows.shape[1], ATT_BLK) - rows.shape[1])


def _mixer_diff(kf, v_rows, zb, pos, past, params, lam_init):
    lq1, lk1, lq2, lk2, norm_g = params
    bsz, t, _ = kf.shape
    tables = _rope_tables(pos, A_ROT_DIM // 2, ROPE_THETA)
    k_rows, k_rows_b = rope_heads(kf, 0, 2 * A_HEADS, tables, A_ROT_DIM // 2, tt=min(t, 512))
    v_col0 = 2 * A_QK_WIDTH
    lam_vecs = jnp.stack([lq1, lq2, lk1, lk2]).astype(F32)
    g = norm_g.reshape(1, 2 * HEAD).astype(F32)
    if past is None:
        o = diff_attention(zb, k_rows_b, zb, v_col0, tables, lam_vecs, g, tq=ATT_TQ, q_off=0,
                           lam_init=lam_init, heads=2)
    else:
        k_all = _with_past(past[0], k_rows_b)
        v_all = _with_past(past[1], zb[:, :, v_col0:v_col0 + MIX_WIDTH])
        o = diff_attention(zb, k_all, v_all, 0, tables, lam_vecs, g, tq=t, q_off=past[0].shape[1],
                           lam_init=lam_init, heads=3)
    shape = (bsz, t, A_HEADS, 2 * HEAD)
    return o, (k_rows.reshape(shape), v_rows.reshape(shape))


def _mixer_band(k, v, zb, pos, past, params):
    (rel_bias,) = params
    bsz, t, _ = k.shape
    kb = zb[:, :, B_WIDTH:2 * B_WIDTH]
    vb = zb[:, :, 2 * B_WIDTH:3 * B_WIDTH]
    bias_tab = _band_bias_table(rel_bias.astype(F32))
    shape = (bsz, -1, B_HEADS, HEAD)
    if past is None:
        o = band_attention(zb, _pad_rows(kb, B_WINDOW, 0), _pad_rows(vb, B_WINDOW, 0), bias_tab,
                           front_pad=B_WINDOW, heads=4)
        keep = min(B_WINDOW, t)
        state = (k[:, t - keep:].reshape(shape), v[:, t - keep:].reshape(shape))
    else:
        buf_len = past[0].shape[1]
        assert t == CHUNK and buf_len == B_WINDOW
        k_all = jnp.concatenate([past[0].reshape(bsz, buf_len, B_WIDTH), k], axis=1)
        v_all = jnp.concatenate([past[1].reshape(bsz, buf_len, B_WIDTH), v], axis=1)
        q_pad = _pad_rows(zb[:, :, :B_WIDTH], BAND_TQ - t, 0)
        o = band_attention(q_pad, _pad_rows(k_all.astype(BF16), CHUNK, 0), _pad_rows(v_all.astype(BF16), CHUNK, 0),
                           bias_tab, front_pad=CHUNK, heads=6)[:, BAND_TQ - t:]
        state = (k_all[:, t:].reshape(shape), v_all[:, t:].reshape(shape))
    return o, state


def _mixer_mla(z, pos, past, params, w_uq, w_ukv):
    q_norm_g, kv_norm_g = params
    bsz, t, _ = z.shape
    half = C_ROPE_DIM // 2
    tables = _rope_tables(pos, half, C_ROPE_THETA)
    cq, latent, kr = mla_prep(z.reshape(bsz * t, z.shape[-1]), q_norm_g.reshape(1, -1).astype(F32),
                              kv_norm_g.reshape(1, -1).astype(F32), tables, seq_len=t, tt=min(t, 512))
    q = matmul(cq, w_uq, tm=min(bsz * t, 1024), tn=1024, out_dtypes=(BF16,)).reshape(bsz, t, -1)
    latent = latent.reshape(bsz, t, C_KV_RANK)
    kr = kr.reshape(bsz, t, HEAD)
    if past is None:
        lat_all, kr_all, q_off, tq, heads = latent, kr.astype(BF16), 0, ATT_TQ, 4
    else:
        q_off, tq, heads = past[0].shape[1], t, 6
        back = _round_up(q_off + t, ATT_BLK) - q_off - t
        lat_all = _pad_rows(jnp.concatenate([past[0], latent], axis=1), 0, back)
        kr_past = jnp.pad(past[1], ((0, 0), (0, 0), (0, HEAD - C_ROPE_DIM)))
        kr_all = _pad_rows(jnp.concatenate([kr_past, kr], axis=1), 0, back).astype(BF16)
    s_len = lat_all.shape[1]
    kv = matmul(lat_all.reshape(bsz * s_len, C_KV_RANK), w_ukv, tm=_pick(bsz * s_len, (1024, 512, 256)),
                tn=1024, out_dtypes=(BF16,)).reshape(bsz, s_len, -1)
    o = mla_attention(q, kv, kr_all, tables, tq=tq, q_off=q_off, heads=heads)
    return o, (latent, kr[:, :, :C_ROPE_DIM])


def _mixer_stick(k, v, zb, pos, past):
    bsz, t, _ = k.shape
    if past is None:
        o = stick_attention(zb, 0, zb, D_WIDTH, zb, 2 * D_WIDTH, tq=ATT_TQ, q_off=0, heads=4)
    else:
        k_all = _with_past(past[0], zb[:, :, D_WIDTH:2 * D_WIDTH])
        v_all = _with_past(past[1], zb[:, :, 2 * D_WIDTH:3 * D_WIDTH])
        o = stick_attention(zb, 0, k_all, 0, v_all, 0, tq=t, q_off=past[0].shape[1], heads=6)
    shape = (bsz, t, D_HEADS, HEAD)
    return o, (k.reshape(shape), v.reshape(shape))


def _trunk_layer(layer, x, pos, past, mem_kv, conv_state, mix_params, w):
    bsz, t, _ = x.shape
    m = bsz * t
    mixer = layer % N_MIXERS
    x2d = x.reshape(m, D_MODEL)
    z, zb = matmul(x2d, w["w_in"], tm=min(m, 1024), tn=1024, out_dtypes=(F32, BF16))
    z = z.reshape(bsz, t, -1)
    zb = zb.reshape(bsz, t, -1)
    zf = [z] if mixer == 2 else [z[:, :, MIX_WIDTH:2 * MIX_WIDTH], z[:, :, 2 * MIX_WIDTH:3 * MIX_WIDTH]]
    if mixer == 0:
        o_mix, state = _mixer_diff(*zf, zb, pos, past, mix_params, 0.8 - 0.6 * math.exp(-0.3 * layer))
    elif mixer == 1:
        o_mix, state = _mixer_band(*zf, zb, pos, past, mix_params)
    elif mixer == 2:
        o_mix, state = _mixer_mla(*zf, pos, past, mix_params, w["w_uq"], w["w_ukv"])
    else:
        o_mix, state = _mixer_stick(*zf, zb, pos, past)
    q_mem_col0 = C_MEM_COL if mixer == 2 else 3 * MIX_WIDTH
    o_mem = memory_attention(zb, q_mem_col0, mem_kv[0], mem_kv[1], mem_kv[2], mem_kv[3], tq=min(t, 512))
    x1 = proj_ln([o_mix.reshape(m, MIX_WIDTH), o_mem.reshape(m, MEM_WIDTH)], w["w_o"], x2d,
                 w["ln1_g"], w["ln1_b"], tm=512, name="out_proj_ln")
    g = ffn_up(x1, w["w_up"], w["conv_w"], w["conv_b"], conv_state, seq_len=t, tm=512, tn=512)
    x2 = proj_ln([g], w["w_down"], x1, w["ln2_g"], w["ln2_b"], tm=256, name="down_proj_ln")
    return x2.reshape(bsz, t, D_MODEL), state, x1.reshape(bsz, t, D_MODEL)


def _reorder_w_in_c(w):
    a, b = C_KR_COL, C_KR_COL + C_ROPE_DIM
    pad = jnp.zeros((w.shape[0], C_MEM_COL - b), w.dtype)
    return jnp.concatenate([w[:, :b], pad, w[:, b:]], axis=1)


def _reorder_w_uq(w):
    w = w.reshape(C_Q_RANK, C_HEADS, C_NOPE_DIM + C_ROPE_DIM)
    nope = w[:, :, :C_NOPE_DIM].reshape(C_Q_RANK, C_HEADS * HEAD)
    rope = jnp.pad(w[:, :, C_NOPE_DIM:], ((0, 0), (0, 0), (0, HEAD - C_ROPE_DIM)))
    return jnp.concatenate([nope, rope.reshape(C_Q_RANK, C_HEADS * HEAD)], axis=1)


def kernel(x_prompt, x_sample, mem_prompt, cache_a_k, cache_a_v, cache_b_k, cache_b_v, cache_c_latent, cache_c_krope, cache_d_k, cache_d_v, cache_mem_k, cache_mem_v, state_ffn_conv, w_in_a, w_in_b, w_in_c, w_in_d, diff_lambda_q1, diff_lambda_k1, diff_lambda_q2, diff_lambda_k2, diff_norm_g, band_rel_bias, mla_q_norm_g, mla_kv_norm_g, mla_w_uq, mla_w_ukv, w_mem_kv, w_o, ln1_g, ln1_b, w_up, conv_ffn_w, conv_ffn_b, w_down, ln2_g, ln2_b):
    n_p, t_p, _ = x_prompt.shape
    n_s, t_s, _ = x_sample.shape
    past_len = cache_d_k.shape[2]
    pos_p = jnp.arange(t_p)
    pos_s = past_len + jnp.arange(t_s)
    caches_by_type = ((cache_a_k, cache_a_v), (cache_b_k, cache_b_v),
                      (cache_c_latent, cache_c_krope), (cache_d_k, cache_d_v))
    params_by_type = ((diff_lambda_q1, diff_lambda_k1, diff_lambda_q2, diff_lambda_k2, diff_norm_g),
                      (band_rel_bias,), (mla_q_norm_g, mla_kv_norm_g), ())
    states_p = [([], []) for _ in range(N_MIXERS)]
    states_s = [([], []) for _ in range(N_MIXERS)]
    mem_k_p, mem_v_p, conv_p, conv_s = [], [], [], []
    n_mem = mem_prompt.shape[1]
    mem2d = mem_prompt.reshape(n_p * n_mem, D_MODEL)
    zero_state = jnp.zeros((n_p, 2, 2 * D_FF), F32)
    x_p, x_s = x_prompt, x_sample
    w_o_b, w_up_b, w_down_b, w_mem_kv_b = (a.astype(BF16) for a in (w_o, w_up, w_down, w_mem_kv))
    w_in_bf = {0: w_in_a.astype(BF16), 1: w_in_b.astype(BF16), 3: w_in_d.astype(BF16)}
    for i in range(DEPTH):
        mixer, j = i % N_MIXERS, i // N_MIXERS
        w = {
            "w_in": (_reorder_w_in_c(w_in_c[j]).astype(BF16) if mixer == 2 else (w_in_bf[mixer], j)),
            "w_o": (w_o_b, i),
            "ln1_g": ln1_g[i].reshape(1, -1), "ln1_b": ln1_b[i].reshape(1, -1),
            "w_up": (w_up_b, i),
            "conv_w": conv_ffn_w[i], "conv_b": conv_ffn_b[i].reshape(1, -1),
            "w_down": (w_down_b, i),
            "ln2_g": ln2_g[i].reshape(1, -1), "ln2_b": ln2_b[i].reshape(1, -1),
        }
        if mixer == 2:
            w["w_uq"] = _reorder_w_uq(mla_w_uq[j]).astype(BF16)
            w["w_ukv"] = mla_w_ukv[j].astype(BF16)
        mix_params = tuple(p[j] for p in params_by_type[mixer])
        kv_mem, kv_mem_b = matmul(mem2d, (w_mem_kv_b, i), tm=n_p * n_mem, tn=512, out_dtypes=(F32, BF16))
        kv_mem = kv_mem.reshape(n_p, n_mem, 2 * MEM_WIDTH)
        kv_mem_b = kv_mem_b.reshape(n_p, n_mem, 2 * MEM_WIDTH)
        x_p, st_p, x1_p = _trunk_layer(i, x_p, pos_p, None, (kv_mem_b, kv_mem_b, 0, MEM_WIDTH), zero_state,
                                       mix_params, w)
        past = (caches_by_type[mixer][0][j], caches_by_type[mixer][1][j])
        mem_s = (cache_mem_k[i].reshape(n_s, n_mem, MEM_WIDTH), cache_mem_v[i].reshape(n_s, n_mem, MEM_WIDTH), 0, 0)
        x_s, st_s, x1_s = _trunk_layer(i, x_s, pos_s, past, mem_s, state_ffn_conv[i], mix_params, w)
        tails = jnp.concatenate([x1_p[:, t_p - 2:].reshape(2 * n_p, D_MODEL),
                                 x1_s[:, t_s - 2:].reshape(2 * n_s, D_MODEL)], axis=0)
        n_tail = tails.shape[0]
        tails = jnp.pad(tails, ((0, _round_up(n_tail, 16) - n_tail), (0, 0)))
        u_tail = matmul(tails, w["w_up"], tm=tails.shape[0], tn=1024)
        conv_p.append(u_tail[:2 * n_p].reshape(n_p, 2, 2 * D_FF))
        conv_s.append(u_tail[2 * n_p:n_tail].reshape(n_s, 2, 2 * D_FF))
        for a in range(2):
            states_p[mixer][a].append(st_p[a])
            states_s[mixer][a].append(st_s[a])
        mem_k_p.append(kv_mem[:, :, :MEM_WIDTH].reshape(n_p, n_mem, MEM_HEADS, HEAD))
        mem_v_p.append(kv_mem[:, :, MEM_WIDTH:].reshape(n_p, n_mem, MEM_HEADS, HEAD))
    outs = [x_p, x_s]
    for st in states_p:
        outs += [jnp.stack(st[0]), jnp.stack(st[1])]
    outs += [jnp.stack(mem_k_p), jnp.stack(mem_v_p), jnp.stack(conv_p)]
    for st in states_s:
        outs += [jnp.stack(st[0]), jnp.stack(st[1])]
    outs.append(jnp.stack(conv_s))
    return tuple(outs)
```

```python
import functools
import math

import jax
import jax.numpy as jnp
from jax import lax
from jax.experimental import pallas as pl
from jax.experimental.pallas import tpu as pltpu

F32 = jnp.float32
BF16 = jnp.bfloat16

D_MODEL = 2048
DEPTH = 4
CHUNK = 64
CHUNK_SHIFT = 6
N_MIXERS = 4
MIX_WIDTH = 1536
MEM_HEADS = 4
MEM_WIDTH = 512
HEAD = 128
ROPE_THETA = 500000.0

A_HEADS = 6
A_QK_WIDTH = 1536
A_ROT_DIM = 32

B_HEADS = 12
B_WIDTH = 1536
B_WINDOW = 512
B_REL_CLIP = 128
BAND_TQ = 128
BAND_KEYS = B_WINDOW + BAND_TQ
BAND_TABLE = 768

C_HEADS = 12
C_Q_RANK = 768
C_KV_RANK = 512
C_NOPE_DIM = 128
C_ROPE_DIM = 64
C_ROPE_THETA = 10000.0
C_KR_COL = C_Q_RANK + C_KV_RANK
C_MEM_COL = 1536
C_IN_PAD = C_MEM_COL + MEM_WIDTH

D_HEADS = 12
D_WIDTH = 1536

D_FF = 5632
FFN_HALO = 16

ATT_TQ = 256
ATT_BLK = 256
ATT_WIDE = 512

DEEPNORM_ALPHA = (2 * DEPTH) ** 0.25
NORM_EPS = 1e-5
NEG_INF = -1e30
LOG2E = 1.0 / math.log(2.0)
STICK_DEAD_BITS = -160.0

VMEM_LIMIT = 56 * 1024 * 1024


def _params(n_axes):
    return pltpu.CompilerParams(dimension_semantics=("arbitrary",) * n_axes,
                                vmem_limit_bytes=VMEM_LIMIT)


def _pick(n, cands):
    for c in cands:
        if n % c == 0:
            return c
    return n


def _mm_kernel(a_ref, b_ref, *rest):
    *o_refs, abf_ref = rest

    @pl.when(pl.program_id(1) == 0)
    def _():
        abf_ref[...] = a_ref[...].astype(BF16)

    r = jnp.dot(abf_ref[...], b_ref[...], preferred_element_type=F32)
    for o_ref in o_refs:
        o_ref[...] = r.astype(o_ref.dtype)


def _layer_weight(w):
    if isinstance(w, tuple):
        return w[0], (None,), (w[1],)
    return w, (), ()


def matmul(a, b, *, tm, tn, out_dtypes=(F32,)):
    m, k = a.shape
    b, b_blk, b_idx = _layer_weight(b)
    n = b.shape[-1]
    assert m % tm == 0 and n % tn == 0, (a.shape, b.shape, tm, tn)
    outs = pl.pallas_call(
        _mm_kernel,
        grid=(m // tm, n // tn),
        in_specs=[pl.BlockSpec((tm, k), lambda i, j: (i, 0)),
                  pl.BlockSpec(b_blk + (k, tn), lambda i, j: b_idx + (0, j))],
        out_specs=[pl.BlockSpec((tm, tn), lambda i, j: (i, j)) for _ in out_dtypes],
        out_shape=[jax.ShapeDtypeStruct((m, n), dt) for dt in out_dtypes],
        scratch_shapes=[pltpu.VMEM((tm, k), BF16)],
        compiler_params=_params(2),
        name="matmul",
    )(a, b)
    return outs[0] if len(out_dtypes) == 1 else outs


def _layer_norm_rows(y, g, b):
    mu = jnp.mean(y, axis=-1, keepdims=True)
    d = y - mu
    var = jnp.mean(d * d, axis=-1, keepdims=True)
    return d * lax.rsqrt(var + NORM_EPS) * g + b


def _proj_ln_kernel(*refs, n_parts):
    a_refs = refs[:n_parts]
    w_ref, x_ref, g_ref, b_ref, o_ref = refs[n_parts:]
    f, k0 = None, 0
    for a_ref in a_refs:
        k = a_ref.shape[1]
        part = jnp.dot(a_ref[...], w_ref[k0:k0 + k, :], preferred_element_type=F32)
        f = part if f is None else f + part
        k0 += k
    y = DEEPNORM_ALPHA * x_ref[...] + f
    o_ref[...] = _layer_norm_rows(y, g_ref[...], b_ref[...])


def proj_ln(a_parts, w, x, g, b, *, tm, name):
    m = x.shape[0]
    w, w_blk, w_idx = _layer_weight(w)
    k_total, n = w.shape[-2:]
    assert sum(a.shape[1] for a in a_parts) == k_total and m % tm == 0
    row = lambda i: (i, 0)
    fixed = lambda i: (0, 0)
    return pl.pallas_call(
        functools.partial(_proj_ln_kernel, n_parts=len(a_parts)),
        grid=(m // tm,),
        in_specs=([pl.BlockSpec((tm, a.shape[1]), row) for a in a_parts]
                  + [pl.BlockSpec(w_blk + (k_total, n), lambda i: w_idx + (0, 0), pipeline_mode=pl.Buffered(1)),
                     pl.BlockSpec((tm, n), row), pl.BlockSpec((1, n), fixed), pl.BlockSpec((1, n), fixed)]),
        out_specs=pl.BlockSpec((tm, n), row),
        out_shape=jax.ShapeDtypeStruct((m, n), F32),
        compiler_params=_params(1),
        name=name,
    )(*a_parts, w, x, g, b)


def _silu(x):
    return x * (1.0 / (1.0 + jnp.exp(-x)))


def _ffn_up_kernel(x_ref, xh_ref, wg_ref, wv_ref, cwg_ref, cwv_ref, cbg_ref, cbv_ref,
                   sg_ref, sv_ref, o_ref, xcat_ref, *, tiles_per_seq, seq_starts):
    i = pl.program_id(0)
    j = pl.program_id(1)

    @pl.when(j == 0)
    def _():
        xcat_ref[:FFN_HALO, :] = xh_ref[...].astype(BF16)
        xcat_ref[FFN_HALO:, :] = x_ref[...].astype(BF16)

    def conv(u, cw_ref, cb_ref):
        cw = cw_ref[...]
        p1 = pltpu.roll(u, 1, 0)[FFN_HALO:]
        p2 = pltpu.roll(u, 2, 0)[FFN_HALO:]
        return cb_ref[...] + cw[0:1] * p2 + cw[1:2] * p1 + cw[2:3] * u[FFN_HALO:]

    xe = xcat_ref[...]
    ug = jnp.dot(xe, wg_ref[...], preferred_element_type=F32)
    uv = jnp.dot(xe, wv_ref[...], preferred_element_type=F32)
    o_ref[...] = (_silu(conv(ug, cwg_ref, cbg_ref)) * conv(uv, cwv_ref, cbv_ref)).astype(o_ref.dtype)

    def fix_sequence_start(seq, row0):
        rid = lax.broadcasted_iota(jnp.int32, (FFN_HALO, o_ref.shape[1]), 0)

        def conv_head(u, cw_ref, cb_ref, st_ref):
            cw = cw_ref[...]
            cur = u[FFN_HALO + row0:2 * FFN_HALO + row0]
            st = st_ref[seq]
            s0, s1 = st[0:1], st[1:2]
            p1 = jnp.where(rid == 0, s1, pltpu.roll(cur, 1, 0))
            p2 = jnp.where(rid == 0, s0, jnp.where(rid == 1, s1, pltpu.roll(cur, 2, 0)))
            return cb_ref[...] + cw[0:1] * p2 + cw[1:2] * p1 + cw[2:3] * cur

        hg = conv_head(ug, cwg_ref, cbg_ref, sg_ref)
        hv = conv_head(uv, cwv_ref, cbv_ref, sv_ref)
        o_ref[row0:row0 + FFN_HALO, :] = (_silu(hg) * hv).astype(o_ref.dtype)

    if tiles_per_seq > 1:
        pl.when(i % tiles_per_seq == 0)(lambda: fix_sequence_start(0, 0))
    else:
        for seq, row0 in enumerate(seq_starts):
            fix_sequence_start(seq, row0)


def ffn_up(x, w_up, conv_w, conv_b, state, *, seq_len, tm, tn):
    m, k = x.shape
    w_up, w_blk, w_idx = _layer_weight(w_up)
    assert (seq_len % tm == 0 or tm % seq_len == 0) and seq_len % FFN_HALO == 0 and D_FF % tn == 0
    tiles_per_seq = max(seq_len // tm, 1)
    seqs_per_tile = max(tm // seq_len, 1)
    nf = D_FF // tn
    halo_blocks = tm // FFN_HALO
    kern = functools.partial(_ffn_up_kernel, tiles_per_seq=tiles_per_seq,
                             seq_starts=tuple(q * seq_len for q in range(seqs_per_tile)))
    seq_block = lambda i: i * seqs_per_tile // tiles_per_seq // seqs_per_tile
    return pl.pallas_call(
        kern,
        grid=(m // tm, nf),
        in_specs=[
            pl.BlockSpec((tm, k), lambda i, j: (i, 0)),
            pl.BlockSpec((FFN_HALO, k), lambda i, j: (jnp.maximum(i * halo_blocks - 1, 0), 0)),
            pl.BlockSpec(w_blk + (k, tn), lambda i, j: w_idx + (0, j)),
            pl.BlockSpec(w_blk + (k, tn), lambda i, j: w_idx + (0, j + nf)),
            pl.BlockSpec((3, tn), lambda i, j: (0, j)),
            pl.BlockSpec((3, tn), lambda i, j: (0, j + nf)),
            pl.BlockSpec((1, tn), lambda i, j: (0, j)),
            pl.BlockSpec((1, tn), lambda i, j: (0, j + nf)),
            pl.BlockSpec((seqs_per_tile, 2, tn), lambda i, j: (seq_block(i), 0, j)),
            pl.BlockSpec((seqs_per_tile, 2, tn), lambda i, j: (seq_block(i), 0, j + nf)),
        ],
        out_specs=pl.BlockSpec((tm, tn), lambda i, j: (i, j)),
        out_shape=jax.ShapeDtypeStruct((m, D_FF), BF16),
        scratch_shapes=[pltpu.VMEM((tm + FFN_HALO, k), BF16)],
        compiler_params=_params(2),
        name="ffn_up",
    )(x, x, w_up, w_up, conv_w, conv_w, conv_b, conv_b, state, state)


def _rope_tables(pos, half, theta):
    inv_freq = theta ** (-jnp.arange(half, dtype=F32) / half)
    ang = pos.astype(F32)[:, None] * inv_freq[None, :]
    cos, sin = jnp.cos(ang), jnp.sin(ang)
    t = pos.shape[0]
    rest = HEAD - 2 * half
    cos_t = jnp.concatenate([cos, cos, jnp.ones((t, rest), F32)], axis=1)
    sin_up = jnp.concatenate([jnp.zeros((t, half), F32), sin, jnp.zeros((t, rest), F32)], axis=1)
    sin_dn = jnp.concatenate([-sin, jnp.zeros((t, half + rest), F32)], axis=1)
    return cos_t, sin_up, sin_dn


def _rope(x, cos_t, sin_up, sin_dn, half):
    return x * cos_t + pltpu.roll(x, half, 1) * sin_up + pltpu.roll(x, HEAD - half, 1) * sin_dn


def _rope_heads_kernel(x_ref, c_ref, su_ref, sd_ref, o_ref, ob_ref, *, half, n_heads):
    tabs = (c_ref[...], su_ref[...], sd_ref[...])
    for h in range(n_heads):
        cols = slice(h * HEAD, (h + 1) * HEAD)
        r = _rope(x_ref[0, :, cols], *tabs, half)
        o_ref[0, :, cols] = r
        ob_ref[0, :, cols] = r.astype(BF16)


def rope_heads(z, col0, n_heads, tables, half, *, tt):
    bsz, t, _ = z.shape
    w = n_heads * HEAD
    assert col0 % w == 0
    c0 = col0 // w
    tab = pl.BlockSpec((tt, HEAD), lambda b, i: (i, 0))
    out = pl.BlockSpec((1, tt, w), lambda b, i: (b, i, 0))
    return pl.pallas_call(
        functools.partial(_rope_heads_kernel, half=half, n_heads=n_heads),
        grid=(bsz, t // tt),
        in_specs=[pl.BlockSpec((1, tt, w), lambda b, i: (b, i, c0)), tab, tab, tab],
        out_specs=[out, out],
        out_shape=[jax.ShapeDtypeStruct((bsz, t, w), F32), jax.ShapeDtypeStruct((bsz, t, w), BF16)],
        compiler_params=_params(2),
        name="rope_heads",
    )(z, *tables)


def _rms_rows(x, g):
    return x * lax.rsqrt(jnp.mean(x * x, axis=-1, keepdims=True) + NORM_EPS) * g


def _mla_prep_kernel(z_ref, gq_ref, gkv_ref, c_ref, su_ref, sd_ref, cq_ref, lat_ref, kr_ref):
    z = z_ref[...]
    cq_ref[...] = _rms_rows(z[:, :C_Q_RANK], gq_ref[...]).astype(cq_ref.dtype)
    lat_ref[...] = _rms_rows(z[:, C_Q_RANK:C_KR_COL], gkv_ref[...])
    kr = z[:, C_KR_COL:C_KR_COL + HEAD]
    kr_ref[...] = _rope(kr, c_ref[...], su_ref[...], sd_ref[...], C_ROPE_DIM // 2)


def mla_prep(z2d, gq, gkv, tables, *, seq_len, tt):
    m = z2d.shape[0]
    nt = seq_len // tt
    row = lambda i: (i, 0)
    fixed = lambda i: (0, 0)
    tab = pl.BlockSpec((tt, HEAD), lambda i: (i % nt, 0))
    return pl.pallas_call(
        _mla_prep_kernel,
        grid=(m // tt,),
        in_specs=[pl.BlockSpec((tt, C_MEM_COL), row), pl.BlockSpec((1, C_Q_RANK), fixed),
                  pl.BlockSpec((1, C_KV_RANK), fixed), tab, tab, tab],
        out_specs=[pl.BlockSpec((tt, C_Q_RANK), row), pl.BlockSpec((tt, C_KV_RANK), row),
                   pl.BlockSpec((tt, HEAD), row)],
        out_shape=[jax.ShapeDtypeStruct((m, C_Q_RANK), BF16),
                   jax.ShapeDtypeStruct((m, C_KV_RANK), F32),
                   jax.ShapeDtypeStruct((m, HEAD), F32)],
        compiler_params=_params(1),
        name="mla_prep",
    )(z2d, gq, gkv, *tables)


def _nt_dot(a, b):
    return lax.dot_general(a, b, (((1,), (1,)), ((), ())), preferred_element_type=F32)


def _lanes(x, n):
    return x if n == HEAD else jnp.concatenate([x] * (n // HEAD), axis=1)


def _chunk_mask(tq, tk, q_start, k_start):
    qpos = q_start + lax.broadcasted_iota(jnp.int32, (tq, tk), 0)
    kpos = k_start + lax.broadcasted_iota(jnp.int32, (tq, tk), 1)
    return jnp.right_shift(kpos, CHUNK_SHIFT) <= jnp.right_shift(qpos, CHUNK_SHIFT)


def _softmax_steps(scores, values, stats):
    probs, alphas = [], []
    for s, (m_ref, l_ref, _) in zip(scores, stats):
        tk = s.shape[1]
        m_old = m_ref[...]
        m_new = jnp.maximum(m_old, jnp.max(s, axis=-1, keepdims=True))
        alpha = jnp.exp2(m_old - m_new)
        p = jnp.exp2(s - _lanes(m_new, tk))
        psum = p[:, :HEAD]
        for c in range(1, tk // HEAD):
            psum = psum + p[:, c * HEAD:(c + 1) * HEAD]
        l_ref[...] = alpha * l_ref[...] + psum
        m_ref[...] = m_new
        probs.append(p.astype(BF16))
        alphas.append(alpha)
    for p, v, alpha, (_, _, acc_ref) in zip(probs, values, alphas, stats):
        acc_ref[...] = (_lanes(alpha, acc_ref.shape[-1]) * acc_ref[...]
                        + jnp.dot(p, v, preferred_element_type=F32))


def _softmax_scratch(n_chains, tq, dv):
    return [pltpu.VMEM((tq, HEAD), F32), pltpu.VMEM((tq, HEAD), F32), pltpu.VMEM((tq, dv), F32)] * n_chains


def _softmax_stats(scratch_refs):
    stats = [tuple(scratch_refs[3 * c:3 * c + 3]) for c in range(len(scratch_refs) // 3)]
    for m_ref, l_ref, acc_ref in stats:
        m_ref[...] = jnp.full_like(m_ref, NEG_INF)
        l_ref[...] = jnp.zeros_like(l_ref)
        acc_ref[...] = jnp.zeros_like(acc_ref)
    return stats


def _softmax_result(stat):
    _, l_ref, acc_ref = stat
    return acc_ref[...] / jnp.sum(l_ref[...], axis=-1, keepdims=True)


def _causal_key_blocks(block, q_start, tq):
    n_wide = q_start // ATT_WIDE

    def body(j, carry):
        block(pl.multiple_of(j * ATT_WIDE, ATT_WIDE), ATT_WIDE, False)
        return carry

    lax.fori_loop(0, n_wide, body, 0)
    if tq > ATT_BLK:
        block(pl.multiple_of(q_start, ATT_WIDE), ATT_WIDE, True)
        return
    rest = n_wide * ATT_WIDE

    @pl.when(q_start - rest >= ATT_BLK)
    def _():
        block(pl.multiple_of(rest, ATT_BLK), ATT_BLK, False)

    block(pl.multiple_of(q_start, ATT_BLK), ATT_BLK, True)


def _check_tiling(t, tq, q_off, s_len):
    assert t % tq == 0 and q_off % ATT_BLK == 0 and s_len % ATT_BLK == 0
    assert tq in (ATT_BLK, ATT_WIDE) or (t == tq and tq <= ATT_BLK), "own chunks must sit in one masked block"
    assert tq <= ATT_BLK or (q_off % ATT_WIDE == 0 and s_len % ATT_WIDE == 0)
    assert q_off + t <= s_len


def _diff_attn_kernel(q_ref, k_ref, v_ref, c_ref, su_ref, sd_ref, lam_ref, g_ref, o_ref,
                      *scratch_refs, tq, q_off, lam_init, heads):
    i = pl.program_id(2)
    q_start = q_off + i * tq
    scale = HEAD ** -0.5 * LOG2E
    tabs = (c_ref[...], su_ref[...], sd_ref[...])
    n_slots = 2 * heads
    qs = [(_rope(q_ref[0, :, sl * HEAD:(sl + 1) * HEAD].astype(F32), *tabs, A_ROT_DIM // 2) * scale).astype(BF16)
          for sl in range(n_slots)]
    stats = _softmax_stats(scratch_refs)

    def block(start, width, masked):
        rows = pl.ds(start, width)
        scores = [_nt_dot(qs[sl], k_ref[0, rows, sl * HEAD:(sl + 1) * HEAD]) for sl in range(n_slots)]
        if masked:
            mask = _chunk_mask(tq, width, q_start, start)
            scores = [jnp.where(mask, s, NEG_INF) for s in scores]
        values = [v_ref[0, rows, (sl // 2) * 2 * HEAD:(sl // 2 + 1) * 2 * HEAD] for sl in range(n_slots)]
        _softmax_steps(scores, values, stats)

    _causal_key_blocks(block, q_start, tq)

    lam_v = lam_ref[...]
    dots = jnp.sum(lam_v[0:2] * lam_v[2:4], axis=-1, keepdims=True)
    lam = jnp.exp(dots[0:1]) - jnp.exp(dots[1:2]) + lam_init
    for g in range(heads):
        o = _softmax_result(stats[2 * g]) - lam * _softmax_result(stats[2 * g + 1])
        o_ref[0, :, g * 2 * HEAD:(g + 1) * 2 * HEAD] = (
            _rms_rows(o, g_ref[...]) * (1.0 - lam_init)).astype(o_ref.dtype)


def diff_attention(q_src, k_rows, v_src, v_col0, q_tables, lam_vecs, norm_g, *, tq, q_off, lam_init, heads):
    bsz, t, _ = q_src.shape
    s_len = k_rows.shape[1]
    _check_tiling(t, tq, q_off, s_len)
    w = heads * 2 * HEAD
    assert A_HEADS % heads == 0 and v_col0 % w == 0
    vc = v_col0 // w
    tab = pl.BlockSpec((tq, HEAD), lambda b, h, i: (i, 0))
    kern = functools.partial(_diff_attn_kernel, tq=tq, q_off=q_off, lam_init=lam_init, heads=heads)
    return pl.pallas_call(
        kern,
        grid=(bsz, A_HEADS // heads, t // tq),
        in_specs=[pl.BlockSpec((1, tq, w), lambda b, h, i: (b, i, h)),
                  pl.BlockSpec((1, s_len, w), lambda b, h, i: (b, 0, h)),
                  pl.BlockSpec((1, s_len, w), lambda b, h, i: (b, 0, vc + h)),
                  tab, tab, tab,
                  pl.BlockSpec((4, HEAD), lambda b, h, i: (0, 0)),
                  pl.BlockSpec((1, 2 * HEAD), lambda b, h, i: (0, 0))],
        out_specs=pl.BlockSpec((1, tq, w), lambda b, h, i: (b, i, h)),
        out_shape=jax.ShapeDtypeStruct((bsz, t, A_HEADS * 2 * HEAD), BF16),
        scratch_shapes=_softmax_scratch(2 * heads, tq, 2 * HEAD),
        compiler_params=_params(3),
        name="diff_attention",
    )(q_src, k_rows, v_src, *q_tables, lam_vecs, norm_g)


def _mla_attn_kernel(qn_ref, qr_ref, kv_ref, kr_ref, c_ref, su_ref, sd_ref, o_ref,
                     *scratch_refs, tq, q_off, heads):
    i = pl.program_id(2)
    q_start = q_off + i * tq
    scale = (C_NOPE_DIM + C_ROPE_DIM) ** -0.5 * LOG2E
    tabs = (c_ref[...], su_ref[...], sd_ref[...])
    qn = [(qn_ref[0, :, g * HEAD:(g + 1) * HEAD].astype(F32) * scale).astype(BF16) for g in range(heads)]
    qr = [(_rope(qr_ref[0, :, g * HEAD:(g + 1) * HEAD].astype(F32), *tabs, C_ROPE_DIM // 2) * scale).astype(BF16)
          for g in range(heads)]
    qs = [jnp.concatenate([qn[g], qr[g]], axis=1) for g in range(heads)]
    stats = _softmax_stats(scratch_refs)

    def block(start, width, masked):
        rows = pl.ds(start, width)
        kr = kr_ref[0, rows, :]
        scores = [_nt_dot(qs[g], jnp.concatenate([kv_ref[0, rows, 2 * g * HEAD:(2 * g + 1) * HEAD], kr], axis=1))
                  for g in range(heads)]
        if masked:
            mask = _chunk_mask(tq, width, q_start, start)
            scores = [jnp.where(mask, s, NEG_INF) for s in scores]
        values = [kv_ref[0, rows, (2 * g + 1) * HEAD:(2 * g + 2) * HEAD] for g in range(heads)]
        _softmax_steps(scores, values, stats)

    _causal_key_blocks(block, q_start, tq)
    for g in range(heads):
        o_ref[0, :, g * HEAD:(g + 1) * HEAD] = _softmax_result(stats[g]).astype(o_ref.dtype)


def mla_attention(q, kv, kr, q_tables, *, tq, q_off, heads):
    bsz, t, _ = q.shape
    s_len = kv.shape[1]
    _check_tiling(t, tq, q_off, s_len)
    assert C_HEADS % heads == 0
    w = heads * HEAD
    n_groups = C_HEADS // heads
    tab = pl.BlockSpec((tq, HEAD), lambda b, h, i: (i, 0))
    kern = functools.partial(_mla_attn_kernel, tq=tq, q_off=q_off, heads=heads)
    return pl.pallas_call(
        kern,
        grid=(bsz, n_groups, t // tq),
        in_specs=[pl.BlockSpec((1, tq, w), lambda b, h, i: (b, i, h)),
                  pl.BlockSpec((1, tq, w), lambda b, h, i: (b, i, n_groups + h)),
                  pl.BlockSpec((1, s_len, 2 * w), lambda b, h, i: (b, 0, h)),
                  pl.BlockSpec((1, s_len, HEAD), lambda b, h, i: (b, 0, 0)),
                  tab, tab, tab],
        out_specs=pl.BlockSpec((1, tq, w), lambda b, h, i: (b, i, h)),
        out_shape=jax.ShapeDtypeStruct((bsz, t, C_HEADS * HEAD), BF16),
        scratch_shapes=_softmax_scratch(heads, tq, HEAD),
        compiler_params=_params(3),
        name="mla_attention",
    )(q, q, kv, kr, *q_tables)


def _softmax_rows(scores):
    probs, sums = [], []
    for s in scores:
        p = jnp.exp2(s - jnp.max(s, axis=-1, keepdims=True))
        sums.append(jnp.sum(p, axis=-1, keepdims=True))
        probs.append(p.astype(BF16))
    return probs, sums


def _mem_attn_kernel(q_ref, k_ref, v_ref, o_ref):
    col = lambda g: slice(g * HEAD, (g + 1) * HEAD)
    scores = [_nt_dot((q_ref[0, :, col(g)].astype(F32) * (HEAD ** -0.5 * LOG2E)).astype(BF16),
                      k_ref[0, :, col(g)].astype(BF16)) for g in range(MEM_HEADS)]
    probs, sums = _softmax_rows(scores)
    for g in range(MEM_HEADS):
        o = jnp.dot(probs[g], v_ref[0, :, col(g)].astype(BF16), preferred_element_type=F32)
        o_ref[0, :, col(g)] = (o / sums[g]).astype(o_ref.dtype)


def memory_attention(q_src, q_col0, mem_k, mem_v, k_col0, v_col0, *, tq):
    bsz, t, _ = q_src.shape
    n_mem = mem_k.shape[1]
    assert q_col0 % MEM_WIDTH == 0 and k_col0 % MEM_WIDTH == 0 and v_col0 % MEM_WIDTH == 0
    qc, kc, vc = q_col0 // MEM_WIDTH, k_col0 // MEM_WIDTH, v_col0 // MEM_WIDTH
    return pl.pallas_call(
        _mem_attn_kernel,
        grid=(bsz, t // tq),
        in_specs=[pl.BlockSpec((1, tq, MEM_WIDTH), lambda b, i: (b, i, qc)),
                  pl.BlockSpec((1, n_mem, MEM_WIDTH), lambda b, i: (b, 0, kc)),
                  pl.BlockSpec((1, n_mem, MEM_WIDTH), lambda b, i: (b, 0, vc))],
        out_specs=pl.BlockSpec((1, tq, MEM_WIDTH), lambda b, i: (b, i, 0)),
        out_shape=jax.ShapeDtypeStruct((bsz, t, MEM_WIDTH), BF16),
        compiler_params=_params(2),
        name="memory_attention",
    )(q_src, mem_k, mem_v)


def _band_attn_kernel(q_ref, k_ref, v_ref, e_ref, o_ref, bias_ref, *, front_pad, heads):
    i = pl.program_id(2)

    @pl.when(i == 0)
    def _():
        for g in range(heads):
            e = jnp.broadcast_to(e_ref[g] * LOG2E, (BAND_TQ, BAND_TABLE))
            bias_ref[g] = pltpu.roll(e, 0, 1, stride=1, stride_axis=0)[:, :BAND_KEYS]

    rows = pl.ds(pl.multiple_of(i * BAND_TQ, BAND_TQ), BAND_KEYS)
    qq = lax.broadcasted_iota(jnp.int32, (BAND_TQ, BAND_KEYS), 0)
    kk = lax.broadcasted_iota(jnp.int32, (BAND_TQ, BAND_KEYS), 1)
    lo = jnp.maximum(front_pad - i * BAND_TQ, jnp.where(qq < CHUNK, 0, CHUNK))
    hi = jnp.where(qq < CHUNK, BAND_KEYS - CHUNK, BAND_KEYS)
    mask = (kk >= lo) & (kk < hi)
    col = lambda g: slice(g * HEAD, (g + 1) * HEAD)
    scores = [_nt_dot((q_ref[0, :, col(g)].astype(F32) * (HEAD ** -0.5 * LOG2E)).astype(BF16),
                      k_ref[0, rows, col(g)]) for g in range(heads)]
    scores = [jnp.where(mask, s + bias_ref[g], NEG_INF) for g, s in enumerate(scores)]
    probs, sums = _softmax_rows(scores)
    for g in range(heads):
        o = jnp.dot(probs[g], v_ref[0, rows, col(g)], preferred_element_type=F32)
        o_ref[0, :, col(g)] = (o / sums[g]).astype(o_ref.dtype)


def band_attention(q_src, k_pad, v_pad, bias_tab, *, front_pad, heads):
    bsz, t, _ = q_src.shape
    s_len = k_pad.shape[1]
    assert t % BAND_TQ == 0 and s_len == t + B_WINDOW and B_HEADS % heads == 0
    w = heads * HEAD
    kern = functools.partial(_band_attn_kernel, front_pad=front_pad, heads=heads)
    return pl.pallas_call(
        kern,
        grid=(bsz, B_HEADS // heads, t // BAND_TQ),
        in_specs=[pl.BlockSpec((1, BAND_TQ, w), lambda b, h, i: (b, i, h)),
                  pl.BlockSpec((1, s_len, w), lambda b, h, i: (b, 0, h)),
                  pl.BlockSpec((1, s_len, w), lambda b, h, i: (b, 0, h)),
                  pl.BlockSpec((heads, 1, BAND_TABLE), lambda b, h, i: (h, 0, 0))],
        out_specs=pl.BlockSpec((1, BAND_TQ, w), lambda b, h, i: (b, i, h)),
        out_shape=jax.ShapeDtypeStruct((bsz, t, B_WIDTH), BF16),
        scratch_shapes=[pltpu.VMEM((heads, BAND_TQ, BAND_KEYS), F32)],
        compiler_params=_params(3),
        name="band_attention",
    )(q_src, k_pad, v_pad, bias_tab)


def _band_bias_table(rel_bias):
    c = jnp.arange(BAND_TABLE)
    d = jnp.where(c <= BAND_KEYS, c, c - BAND_TABLE)
    idx = jnp.clip(B_WINDOW - d, -B_REL_CLIP, B_REL_CLIP) + B_REL_CLIP
    return rel_bias[:, idx][:, None, :]


def _stick_attn_kernel(q_ref, k_ref, v_ref, o_ref, tri_ref, *state_refs, tq, q_off, heads):
    acc_refs, run_refs = state_refs[:heads], state_refs[heads:]
    i = pl.program_id(2)
    q_start = q_off + i * tq
    tk = ATT_BLK

    @pl.when((pl.program_id(0) == 0) & (pl.program_id(1) == 0) & (i == 0))
    def _():
        r = lax.broadcasted_iota(jnp.int32, (2 * tk, tk), 0)
        c = lax.broadcasted_iota(jnp.int32, (2 * tk, tk), 1)
        tri_ref[...] = jnp.where(jnp.where(r >= tk, r - tk, r) > c, 1.0, 0.0).astype(BF16)

    qs = [(q_ref[0, :, g * HEAD:(g + 1) * HEAD].astype(F32) * (HEAD ** -0.5 * LOG2E)).astype(BF16)
          for g in range(heads)]
    for ref in state_refs:
        ref[...] = jnp.zeros_like(ref)

    def block(start, masked):
        rows = pl.ds(start, tk)
        if masked:
            qpos = q_start + lax.broadcasted_iota(jnp.int32, (tq, tk), 0)
            kpos = start + lax.broadcasted_iota(jnp.int32, (tq, tk), 1)
            allowed = kpos < qpos
        tri = tri_ref[...]
        col = lambda g: slice(g * HEAD, (g + 1) * HEAD)
        zs = [_nt_dot(qs[g], k_ref[0, rows, col(g)]) for g in range(heads)]
        log_betas, log_1m_betas = [], []
        for z in zs:
            log_beta = jnp.minimum(z, 0.0) - jnp.log2(1.0 + jnp.exp2(-jnp.abs(z)))
            log_1m_beta = log_beta - z
            log_betas.append(log_beta)
            log_1m_betas.append(jnp.where(allowed, log_1m_beta, 0.0) if masked else log_1m_beta)
        tails = []
        for g, log_1m_beta in enumerate(log_1m_betas):
            hi = log_1m_beta.astype(BF16)
            lo = (log_1m_beta - hi.astype(F32)).astype(BF16)
            tails.append(jnp.dot(jnp.concatenate([hi, lo], axis=1), tri, preferred_element_type=F32)
                         + run_refs[g][...])
        for g in range(heads):
            a = jnp.exp2(log_betas[g] + tails[g])
            if masked:
                a = jnp.where(allowed, a, 0.0)
            acc_refs[g][...] += jnp.dot(a.astype(BF16), v_ref[0, rows, col(g)], preferred_element_type=F32)
            run_refs[g][...] += jnp.sum(log_1m_betas[g], axis=-1, keepdims=True)

    block(pl.multiple_of(q_start, tk), True)
    n_before = q_start // tk

    def any_stick_left():
        top = run_refs[0][...]
        for g in range(1, heads):
            top = jnp.maximum(top, run_refs[g][...])
        return (jnp.max(top) > STICK_DEAD_BITS).astype(jnp.int32)

    def more(carry):
        step, live = carry
        return (step < n_before) & (live > 0)

    def body(carry):
        step, _ = carry
        block(pl.multiple_of((n_before - 1 - step) * tk, tk), False)
        return step + 1, any_stick_left()

    lax.while_loop(more, body, (jnp.int32(0), any_stick_left()))
    for g in range(heads):
        o_ref[0, :, g * HEAD:(g + 1) * HEAD] = acc_refs[g][...].astype(o_ref.dtype)


def stick_attention(src, q_col0, k_rows, k_col0, v_rows, v_col0, *, tq, q_off, heads):
    bsz, t, _ = src.shape
    s_len = k_rows.shape[1]
    _check_tiling(t, tq, q_off, s_len)
    w = heads * HEAD
    assert D_HEADS % heads == 0 and q_col0 % w == 0 and k_col0 % w == 0 and v_col0 % w == 0
    qc, kc, vc = q_col0 // w, k_col0 // w, v_col0 // w
    kern = functools.partial(_stick_attn_kernel, tq=tq, q_off=q_off, heads=heads)
    return pl.pallas_call(
        kern,
        grid=(bsz, D_HEADS // heads, t // tq),
        in_specs=[pl.BlockSpec((1, tq, w), lambda b, h, i: (b, i, qc + h)),
                  pl.BlockSpec((1, s_len, w), lambda b, h, i: (b, 0, kc + h)),
                  pl.BlockSpec((1, s_len, w), lambda b, h, i: (b, 0, vc + h))],
        out_specs=pl.BlockSpec((1, tq, w), lambda b, h, i: (b, i, h)),
        out_shape=jax.ShapeDtypeStruct((bsz, t, D_WIDTH), BF16),
        scratch_shapes=([pltpu.VMEM((2 * ATT_BLK, ATT_BLK), BF16)] + [pltpu.VMEM((tq, HEAD), F32)] * heads
                        + [pltpu.VMEM((tq, 1), F32)] * heads),
        compiler_params=_params(3),
        name="stick_attention",
    )(src, k_rows, v_rows)


def _pad_rows(a, front, back):
    return jnp.pad(a, ((0, 0), (front, back), (0, 0)))


def _round_up(n, mult):
    return (n + mult - 1) // mult * mult


def _with_past(past, new):
    bsz, p_len = past.shape[0], past.shape[1]
    rows = jnp.concatenate([past.reshape(bsz, p_len, new.shape[2]).astype(BF16), new], axis=1)
    return _pad_rows(rows, 0, _round_up(rows.shape[1], ATT_BLK) - rows.shape[1])


def _mixer_diff(kf, v_rows, zb, pos, past, params, lam_init):
    lq1, lk1, lq2, lk2, norm_g = params
    bsz, t, _ = kf.shape
    tables = _rope_tables(pos, A_ROT_DIM // 2, ROPE_THETA)
    k_rows, k_rows_b = rope_heads(kf, 0, 2 * A_HEADS, tables, A_ROT_DIM // 2, tt=min(t, 512))
    v_col0 = 2 * A_QK_WIDTH
    lam_vecs = jnp.stack([lq1, lq2, lk1, lk2]).astype(F32)
    g = norm_g.reshape(1, 2 * HEAD).astype(F32)
    if past is None:
        o = diff_attention(zb, k_rows_b, zb, v_col0, tables, lam_vecs, g, tq=ATT_TQ, q_off=0,
                           lam_init=lam_init, heads=2)
    else:
        k_all = _with_past(past[0], k_rows_b)
        v_all = _with_past(past[1], zb[:, :, v_col0:v_col0 + MIX_WIDTH])
        o = diff_attention(zb, k_all, v_all, 0, tables, lam_vecs, g, tq=t, q_off=past[0].shape[1],
                           lam_init=lam_init, heads=3)
    shape = (bsz, t, A_HEADS, 2 * HEAD)
    return o, (k_rows.reshape(shape), v_rows.reshape(shape))


def _mixer_band(k, v, zb, pos, past, params):
    (rel_bias,) = params
    bsz, t, _ = k.shape
    kb = zb[:, :, B_WIDTH:2 * B_WIDTH]
    vb = zb[:, :, 2 * B_WIDTH:3 * B_WIDTH]
    bias_tab = _band_bias_table(rel_bias.astype(F32))
    shape = (bsz, -1, B_HEADS, HEAD)
    if past is None:
        o = band_attention(zb, _pad_rows(kb, B_WINDOW, 0), _pad_rows(vb, B_WINDOW, 0), bias_tab,
                           front_pad=B_WINDOW, heads=4)
        keep = min(B_WINDOW, t)
        state = (k[:, t - keep:].reshape(shape), v[:, t - keep:].reshape(shape))
    else:
        buf_len = past[0].shape[1]
        assert t == CHUNK and buf_len == B_WINDOW
        k_all = jnp.concatenate([past[0].reshape(bsz, buf_len, B_WIDTH), k], axis=1)
        v_all = jnp.concatenate([past[1].reshape(bsz, buf_len, B_WIDTH), v], axis=1)
        q_pad = _pad_rows(zb[:, :, :B_WIDTH], BAND_TQ - t, 0)
        o = band_attention(q_pad, _pad_rows(k_all.astype(BF16), CHUNK, 0), _pad_rows(v_all.astype(BF16), CHUNK, 0),
                           bias_tab, front_pad=CHUNK, heads=6)[:, BAND_TQ - t:]
        state = (k_all[:, t:].reshape(shape), v_all[:, t:].reshape(shape))
    return o, state


def _mixer_mla(z, pos, past, params, w_uq, w_ukv):
    q_norm_g, kv_norm_g = params
    bsz, t, _ = z.shape
    half = C_ROPE_DIM // 2
    tables = _rope_tables(pos, half, C_ROPE_THETA)
    cq, latent, kr = mla_prep(z.reshape(bsz * t, z.shape[-1]), q_norm_g.reshape(1, -1).astype(F32),
                              kv_norm_g.reshape(1, -1).astype(F32), tables, seq_len=t, tt=min(t, 512))
    q = matmul(cq, w_uq, tm=min(bsz * t, 1024), tn=1024, out_dtypes=(BF16,)).reshape(bsz, t, -1)
    latent = latent.reshape(bsz, t, C_KV_RANK)
    kr = kr.reshape(bsz, t, HEAD)
    if past is None:
        lat_all, kr_all, q_off, tq, heads = latent, kr.astype(BF16), 0, ATT_WIDE, 4
    else:
        q_off, tq, heads = past[0].shape[1], t, 6
        back = _round_up(q_off + t, ATT_BLK) - q_off - t
        lat_all = _pad_rows(jnp.concatenate([past[0], latent], axis=1), 0, back)
        kr_past = jnp.pad(past[1], ((0, 0), (0, 0), (0, HEAD - C_ROPE_DIM)))
        kr_all = _pad_rows(jnp.concatenate([kr_past, kr], axis=1), 0, back).astype(BF16)
    s_len = lat_all.shape[1]
    kv = matmul(lat_all.reshape(bsz * s_len, C_KV_RANK), w_ukv, tm=_pick(bsz * s_len, (1024, 512, 256)),
                tn=1024, out_dtypes=(BF16,)).reshape(bsz, s_len, -1)
    o = mla_attention(q, kv, kr_all, tables, tq=tq, q_off=q_off, heads=heads)
    return o, (latent, kr[:, :, :C_ROPE_DIM])


def _mixer_stick(k, v, zb, pos, past):
    bsz, t, _ = k.shape
    if past is None:
        o = stick_attention(zb, 0, zb, D_WIDTH, zb, 2 * D_WIDTH, tq=ATT_TQ, q_off=0, heads=4)
    else:
        k_all = _with_past(past[0], zb[:, :, D_WIDTH:2 * D_WIDTH])
        v_all = _with_past(past[1], zb[:, :, 2 * D_WIDTH:3 * D_WIDTH])
        o = stick_attention(zb, 0, k_all, 0, v_all, 0, tq=t, q_off=past[0].shape[1], heads=6)
    shape = (bsz, t, D_HEADS, HEAD)
    return o, (k.reshape(shape), v.reshape(shape))


def _trunk_layer(layer, x, pos, past, mem_kv, conv_state, mix_params, w):
    bsz, t, _ = x.shape
    m = bsz * t
    mixer = layer % N_MIXERS
    x2d = x.reshape(m, D_MODEL)
    z, zb = matmul(x2d, w["w_in"], tm=min(m, 1024), tn=1024, out_dtypes=(F32, BF16))
    z = z.reshape(bsz, t, -1)
    zb = zb.reshape(bsz, t, -1)
    zf = [z] if mixer == 2 else [z[:, :, MIX_WIDTH:2 * MIX_WIDTH], z[:, :, 2 * MIX_WIDTH:3 * MIX_WIDTH]]
    if mixer == 0:
        o_mix, state = _mixer_diff(*zf, zb, pos, past, mix_params, 0.8 - 0.6 * math.exp(-0.3 * layer))
    elif mixer == 1:
        o_mix, state = _mixer_band(*zf, zb, pos, past, mix_params)
    elif mixer == 2:
        o_mix, state = _mixer_mla(*zf, pos, past, mix_params, w["w_uq"], w["w_ukv"])
    else:
        o_mix, state = _mixer_stick(*zf, zb, pos, past)
    q_mem_col0 = C_MEM_COL if mixer == 2 else 3 * MIX_WIDTH
    o_mem = memory_attention(zb, q_mem_col0, mem_kv[0], mem_kv[1], mem_kv[2], mem_kv[3], tq=min(t, 512))
    x1 = proj_ln([o_mix.reshape(m, MIX_WIDTH), o_mem.reshape(m, MEM_WIDTH)], w["w_o"], x2d,
                 w["ln1_g"], w["ln1_b"], tm=512, name="out_proj_ln")
    g = ffn_up(x1, w["w_up"], w["conv_w"], w["conv_b"], conv_state, seq_len=t, tm=512, tn=512)
    x2 = proj_ln([g], w["w_down"], x1, w["ln2_g"], w["ln2_b"], tm=256, name="down_proj_ln")
    return x2.reshape(bsz, t, D_MODEL), state, x1.reshape(bsz, t, D_MODEL)


def _reorder_w_in_c(w):
    a, b = C_KR_COL, C_KR_COL + C_ROPE_DIM
    pad = jnp.zeros((w.shape[0], C_MEM_COL - b), w.dtype)
    return jnp.concatenate([w[:, :b], pad, w[:, b:]], axis=1)


def _reorder_w_uq(w):
    w = w.reshape(C_Q_RANK, C_HEADS, C_NOPE_DIM + C_ROPE_DIM)
    nope = w[:, :, :C_NOPE_DIM].reshape(C_Q_RANK, C_HEADS * HEAD)
    rope = jnp.pad(w[:, :, C_NOPE_DIM:], ((0, 0), (0, 0), (0, HEAD - C_ROPE_DIM)))
    return jnp.concatenate([nope, rope.reshape(C_Q_RANK, C_HEADS * HEAD)], axis=1)


def kernel(x_prompt, x_sample, mem_prompt, cache_a_k, cache_a_v, cache_b_k, cache_b_v, cache_c_latent, cache_c_krope, cache_d_k, cache_d_v, cache_mem_k, cache_mem_v, state_ffn_conv, w_in_a, w_in_b, w_in_c, w_in_d, diff_lambda_q1, diff_lambda_k1, diff_lambda_q2, diff_lambda_k2, diff_norm_g, band_rel_bias, mla_q_norm_g, mla_kv_norm_g, mla_w_uq, mla_w_ukv, w_mem_kv, w_o, ln1_g, ln1_b, w_up, conv_ffn_w, conv_ffn_b, w_down, ln2_g, ln2_b):
    n_p, t_p, _ = x_prompt.shape
    n_s, t_s, _ = x_sample.shape
    past_len = cache_d_k.shape[2]
    pos_p = jnp.arange(t_p)
    pos_s = past_len + jnp.arange(t_s)
    caches_by_type = ((cache_a_k, cache_a_v), (cache_b_k, cache_b_v),
                      (cache_c_latent, cache_c_krope), (cache_d_k, cache_d_v))
    params_by_type = ((diff_lambda_q1, diff_lambda_k1, diff_lambda_q2, diff_lambda_k2, diff_norm_g),
                      (band_rel_bias,), (mla_q_norm_g, mla_kv_norm_g), ())
    states_p = [([], []) for _ in range(N_MIXERS)]
    states_s = [([], []) for _ in range(N_MIXERS)]
    mem_k_p, mem_v_p, conv_p, conv_s = [], [], [], []
    n_mem = mem_prompt.shape[1]
    mem2d = mem_prompt.reshape(n_p * n_mem, D_MODEL)
    zero_state = jnp.zeros((n_p, 2, 2 * D_FF), F32)
    x_p, x_s = x_prompt, x_sample
    w_o_b, w_up_b, w_down_b, w_mem_kv_b = (a.astype(BF16) for a in (w_o, w_up, w_down, w_mem_kv))
    w_in_bf = {0: w_in_a.astype(BF16), 1: w_in_b.astype(BF16), 3: w_in_d.astype(BF16)}
    for i in range(DEPTH):
        mixer, j = i % N_MIXERS, i // N_MIXERS
        w = {
            "w_in": (_reorder_w_in_c(w_in_c[j]).astype(BF16) if mixer == 2 else (w_in_bf[mixer], j)),
            "w_o": (w_o_b, i),
            "ln1_g": ln1_g[i].reshape(1, -1), "ln1_b": ln1_b[i].reshape(1, -1),
            "w_up": (w_up_b, i),
            "conv_w": conv_ffn_w[i], "conv_b": conv_ffn_b[i].reshape(1, -1),
            "w_down": (w_down_b, i),
            "ln2_g": ln2_g[i].reshape(1, -1), "ln2_b": ln2_b[i].reshape(1, -1),
        }
        if mixer == 2:
            w["w_uq"] = _reorder_w_uq(mla_w_uq[j]).astype(BF16)
            w["w_ukv"] = mla_w_ukv[j].astype(BF16)
        mix_params = tuple(p[j] for p in params_by_type[mixer])
        kv_mem, kv_mem_b = matmul(mem2d, (w_mem_kv_b, i), tm=n_p * n_mem, tn=512, out_dtypes=(F32, BF16))
        kv_mem = kv_mem.reshape(n_p, n_mem, 2 * MEM_WIDTH)
        kv_mem_b = kv_mem_b.reshape(n_p, n_mem, 2 * MEM_WIDTH)
        x_p, st_p, x1_p = _trunk_layer(i, x_p, pos_p, None, (kv_mem_b, kv_mem_b, 0, MEM_WIDTH), zero_state,
                                       mix_params, w)
        past = (caches_by_type[mixer][0][j], caches_by_type[mixer][1][j])
        mem_s = (cache_mem_k[i].reshape(n_s, n_mem, MEM_WIDTH), cache_mem_v[i].reshape(n_s, n_mem, MEM_WIDTH), 0, 0)
        x_s, st_s, x1_s = _trunk_layer(i, x_s, pos_s, past, mem_s, state_ffn_conv[i], mix_params, w)
        tails = jnp.concatenate([x1_p[:, t_p - 2:].reshape(2 * n_p, D_MODEL),
                                 x1_s[:, t_s - 2:].reshape(2 * n_s, D_MODEL)], axis=0)
        n_tail = tails.shape[0]
        tails = jnp.pad(tails, ((0, _round_up(n_tail, 16) - n_tail), (0, 0)))
        u_tail = matmul(tails, w["w_up"], tm=tails.shape[0], tn=1024)
        conv_p.append(u_tail[:2 * n_p].reshape(n_p, 2, 2 * D_FF))
        conv_s.append(u_tail[2 * n_p:n_tail].reshape(n_s, 2, 2 * D_FF))
        for a in range(2):
            states_p[mixer][a].append(st_p[a])
            states_s[mixer][a].append(st_s[a])
        mem_k_p.append(kv_mem[:, :, :MEM_WIDTH].reshape(n_p, n_mem, MEM_HEADS, HEAD))
        mem_v_p.append(kv_mem[:, :, MEM_WIDTH:].reshape(n_p, n_mem, MEM_HEADS, HEAD))
    outs = [x_p, x_s]
    for st in states_p:
        outs += [jnp.stack(st[0]), jnp.stack(st[1])]
    outs += [jnp.stack(mem_k_p), jnp.stack(mem_v_p), jnp.stack(conv_p)]
    for st in states_s:
        outs += [jnp.stack(st[0]), jnp.stack(st[1])]
    outs.append(jnp.stack(conv_s))
    return tuple(outs)
```

```python
import functools
import math

import jax
import jax.numpy as jnp
from jax import lax
from jax.experimental import pallas as pl
from jax.experimental.pallas import tpu as pltpu

F32 = jnp.float32
BF16 = jnp.bfloat16

D_MODEL = 2048
DEPTH = 4
CHUNK = 64
CHUNK_SHIFT = 6
N_MIXERS = 4
MIX_WIDTH = 1536
MEM_HEADS = 4
MEM_WIDTH = 512
HEAD = 128
ROPE_THETA = 500000.0

A_HEADS = 6
A_QK_WIDTH = 1536
A_ROT_DIM = 32

B_HEADS = 12
B_WIDTH = 1536
B_WINDOW = 512
B_REL_CLIP = 128
BAND_TQ = 128
BAND_KEYS = B_WINDOW + BAND_TQ
BAND_TABLE = 768

C_HEADS = 12
C_Q_RANK = 768
C_KV_RANK = 512
C_NOPE_DIM = 128
C_ROPE_DIM = 64
C_ROPE_THETA = 10000.0
C_KR_COL = C_Q_RANK + C_KV_RANK
C_MEM_COL = 1536
C_IN_PAD = C_MEM_COL + MEM_WIDTH

D_HEADS = 12
D_WIDTH = 1536

D_FF = 5632
FFN_HALO = 16

ATT_TQ = 256
ATT_BLK = 256
ATT_WIDE = 512

DEEPNORM_ALPHA = (2 * DEPTH) ** 0.25
NORM_EPS = 1e-5
NEG_INF = -1e30
LOG2E = 1.0 / math.log(2.0)
STICK_DEAD_BITS = -160.0

VMEM_LIMIT = 56 * 1024 * 1024


def _params(n_axes):
    return pltpu.CompilerParams(dimension_semantics=("arbitrary",) * n_axes,
                                vmem_limit_bytes=VMEM_LIMIT)


def _pick(n, cands):
    for c in cands:
        if n % c == 0:
            return c
    return n


def _mm_kernel(a_ref, b_ref, *rest):
    *o_refs, abf_ref = rest

    @pl.when(pl.program_id(1) == 0)
    def _():
        abf_ref[...] = a_ref[...].astype(BF16)

    r = jnp.dot(abf_ref[...], b_ref[...], preferred_element_type=F32)
    for o_ref in o_refs:
        o_ref[...] = r.astype(o_ref.dtype)


def _layer_weight(w):
    if isinstance(w, tuple):
        return w[0], (None,), (w[1],)
    return w, (), ()


def matmul(a, b, *, tm, tn, out_dtypes=(F32,)):
    m, k = a.shape
    b, b_blk, b_idx = _layer_weight(b)
    n = b.shape[-1]
    assert m % tm == 0 and n % tn == 0, (a.shape, b.shape, tm, tn)
    outs = pl.pallas_call(
        _mm_kernel,
        grid=(m // tm, n // tn),
        in_specs=[pl.BlockSpec((tm, k), lambda i, j: (i, 0)),
                  pl.BlockSpec(b_blk + (k, tn), lambda i, j: b_idx + (0, j))],
        out_specs=[pl.BlockSpec((tm, tn), lambda i, j: (i, j)) for _ in out_dtypes],
        out_shape=[jax.ShapeDtypeStruct((m, n), dt) for dt in out_dtypes],
        scratch_shapes=[pltpu.VMEM((tm, k), BF16)],
        compiler_params=_params(2),
        name="matmul",
    )(a, b)
    return outs[0] if len(out_dtypes) == 1 else outs


def _layer_norm_rows(y, g, b):
    mu = jnp.mean(y, axis=-1, keepdims=True)
    d = y - mu
    var = jnp.mean(d * d, axis=-1, keepdims=True)
    return d * lax.rsqrt(var + NORM_EPS) * g + b


def _proj_ln_kernel(*refs, n_parts):
    a_refs = refs[:n_parts]
    w_ref, x_ref, g_ref, b_ref, o_ref = refs[n_parts:]
    f, k0 = None, 0
    for a_ref in a_refs:
        k = a_ref.shape[1]
        part = jnp.dot(a_ref[...], w_ref[k0:k0 + k, :], preferred_element_type=F32)
        f = part if f is None else f + part
        k0 += k
    y = DEEPNORM_ALPHA * x_ref[...] + f
    o_ref[...] = _layer_norm_rows(y, g_ref[...], b_ref[...])


def proj_ln(a_parts, w, x, g, b, *, tm, name):
    m = x.shape[0]
    w, w_blk, w_idx = _layer_weight(w)
    k_total, n = w.shape[-2:]
    assert sum(a.shape[1] for a in a_parts) == k_total and m % tm == 0
    row = lambda i: (i, 0)
    fixed = lambda i: (0, 0)
    return pl.pallas_call(
        functools.partial(_proj_ln_kernel, n_parts=len(a_parts)),
        grid=(m // tm,),
        in_specs=([pl.BlockSpec((tm, a.shape[1]), row) for a in a_parts]
                  + [pl.BlockSpec(w_blk + (k_total, n), lambda i: w_idx + (0, 0), pipeline_mode=pl.Buffered(1)),
                     pl.BlockSpec((tm, n), row), pl.BlockSpec((1, n), fixed), pl.BlockSpec((1, n), fixed)]),
        out_specs=pl.BlockSpec((tm, n), row),
        out_shape=jax.ShapeDtypeStruct((m, n), F32),
        compiler_params=_params(1),
        name=name,
    )(*a_parts, w, x, g, b)


def _silu(x):
    return x * (1.0 / (1.0 + jnp.exp(-x)))


def _ffn_up_kernel(x_ref, xh_ref, wg_ref, wv_ref, cwg_ref, cwv_ref, cbg_ref, cbv_ref,
                   sg_ref, sv_ref, o_ref, xcat_ref, *, tiles_per_seq, seq_starts):
    i = pl.program_id(0)
    j = pl.program_id(1)

    @pl.when(j == 0)
    def _():
        xcat_ref[:FFN_HALO, :] = xh_ref[...].astype(BF16)
        xcat_ref[FFN_HALO:, :] = x_ref[...].astype(BF16)

    def conv(u, cw_ref, cb_ref):
        cw = cw_ref[...]
        p1 = pltpu.roll(u, 1, 0)[FFN_HALO:]
        p2 = pltpu.roll(u, 2, 0)[FFN_HALO:]
        return cb_ref[...] + cw[0:1] * p2 + cw[1:2] * p1 + cw[2:3] * u[FFN_HALO:]

    xe = xcat_ref[...]
    ug = jnp.dot(xe, wg_ref[...], preferred_element_type=F32)
    uv = jnp.dot(xe, wv_ref[...], preferred_element_type=F32)
    o_ref[...] = (_silu(conv(ug, cwg_ref, cbg_ref)) * conv(uv, cwv_ref, cbv_ref)).astype(o_ref.dtype)

    def fix_sequence_start(seq, row0):
        rid = lax.broadcasted_iota(jnp.int32, (FFN_HALO, o_ref.shape[1]), 0)

        def conv_head(u, cw_ref, cb_ref, st_ref):
            cw = cw_ref[...]
            cur = u[FFN_HALO + row0:2 * FFN_HALO + row0]
            st = st_ref[seq]
            s0, s1 = st[0:1], st[1:2]
            p1 = jnp.where(rid == 0, s1, pltpu.roll(cur, 1, 0))
            p2 = jnp.where(rid == 0, s0, jnp.where(rid == 1, s1, pltpu.roll(cur, 2, 0)))
            return cb_ref[...] + cw[0:1] * p2 + cw[1:2] * p1 + cw[2:3] * cur

        hg = conv_head(ug, cwg_ref, cbg_ref, sg_ref)
        hv = conv_head(uv, cwv_ref, cbv_ref, sv_ref)
        o_ref[row0:row0 + FFN_HALO, :] = (_silu(hg) * hv).astype(o_ref.dtype)

    if tiles_per_seq > 1:
        pl.when(i % tiles_per_seq == 0)(lambda: fix_sequence_start(0, 0))
    else:
        for seq, row0 in enumerate(seq_starts):
            fix_sequence_start(seq, row0)


def ffn_up(x, w_up, conv_w, conv_b, state, *, seq_len, tm, tn):
    m, k = x.shape
    w_up, w_blk, w_idx = _layer_weight(w_up)
    assert (seq_len % tm == 0 or tm % seq_len == 0) and seq_len % FFN_HALO == 0 and D_FF % tn == 0
    tiles_per_seq = max(seq_len // tm, 1)
    seqs_per_tile = max(tm // seq_len, 1)
    nf = D_FF // tn
    halo_blocks = tm // FFN_HALO
    kern = functools.partial(_ffn_up_kernel, tiles_per_seq=tiles_per_seq,
                             seq_starts=tuple(q * seq_len for q in range(seqs_per_tile)))
    seq_block = lambda i: i * seqs_per_tile // tiles_per_seq // seqs_per_tile
    return pl.pallas_call(
        kern,
        grid=(m // tm, nf),
        in_specs=[
            pl.BlockSpec((tm, k), lambda i, j: (i, 0)),
            pl.BlockSpec((FFN_HALO, k), lambda i, j: (jnp.maximum(i * halo_blocks - 1, 0), 0)),
            pl.BlockSpec(w_blk + (k, tn), lambda i, j: w_idx + (0, j)),
            pl.BlockSpec(w_blk + (k, tn), lambda i, j: w_idx + (0, j + nf)),
            pl.BlockSpec((3, tn), lambda i, j: (0, j)),
            pl.BlockSpec((3, tn), lambda i, j: (0, j + nf)),
            pl.BlockSpec((1, tn), lambda i, j: (0, j)),
            pl.BlockSpec((1, tn), lambda i, j: (0, j + nf)),
            pl.BlockSpec((seqs_per_tile, 2, tn), lambda i, j: (seq_block(i), 0, j)),
            pl.BlockSpec((seqs_per_tile, 2, tn), lambda i, j: (seq_block(i), 0, j + nf)),
        ],
        out_specs=pl.BlockSpec((tm, tn), lambda i, j: (i, j)),
        out_shape=jax.ShapeDtypeStruct((m, D_FF), BF16),
        scratch_shapes=[pltpu.VMEM((tm + FFN_HALO, k), BF16)],
        compiler_params=_params(2),
        name="ffn_up",
    )(x, x, w_up, w_up, conv_w, conv_w, conv_b, conv_b, state, state)


def _rope_tables(pos, half, theta):
    inv_freq = theta ** (-jnp.arange(half, dtype=F32) / half)
    ang = pos.astype(F32)[:, None] * inv_freq[None, :]
    cos, sin = jnp.cos(ang), jnp.sin(ang)
    t = pos.shape[0]
    rest = HEAD - 2 * half
    cos_t = jnp.concatenate([cos, cos, jnp.ones((t, rest), F32)], axis=1)
    sin_up = jnp.concatenate([jnp.zeros((t, half), F32), sin, jnp.zeros((t, rest), F32)], axis=1)
    sin_dn = jnp.concatenate([-sin, jnp.zeros((t, half + rest), F32)], axis=1)
    return cos_t, sin_up, sin_dn


def _rope(x, cos_t, sin_up, sin_dn, half):
    return x * cos_t + pltpu.roll(x, half, 1) * sin_up + pltpu.roll(x, HEAD - half, 1) * sin_dn


def _rope_heads_kernel(x_ref, c_ref, su_ref, sd_ref, o_ref, ob_ref, *, half, n_heads):
    tabs = (c_ref[...], su_ref[...], sd_ref[...])
    for h in range(n_heads):
        cols = slice(h * HEAD, (h + 1) * HEAD)
        r = _rope(x_ref[0, :, cols], *tabs, half)
        o_ref[0, :, cols] = r
        ob_ref[0, :, cols] = r.astype(BF16)


def rope_heads(z, col0, n_heads, tables, half, *, tt):
    bsz, t, _ = z.shape
    w = n_heads * HEAD
    assert col0 % w == 0
    c0 = col0 // w
    tab = pl.BlockSpec((tt, HEAD), lambda b, i: (i, 0))
    out = pl.BlockSpec((1, tt, w), lambda b, i: (b, i, 0))
    return pl.pallas_call(
        functools.partial(_rope_heads_kernel, half=half, n_heads=n_heads),
        grid=(bsz, t // tt),
        in_specs=[pl.BlockSpec((1, tt, w), lambda b, i: (b, i, c0)), tab, tab, tab],
        out_specs=[out, out],
        out_shape=[jax.ShapeDtypeStruct((bsz, t, w), F32), jax.ShapeDtypeStruct((bsz, t, w), BF16)],
        compiler_params=_params(2),
        name="rope_heads",
    )(z, *tables)


def _rms_rows(x, g):
    return x * lax.rsqrt(jnp.mean(x * x, axis=-1, keepdims=True) + NORM_EPS) * g


def _mla_prep_kernel(z_ref, gq_ref, gkv_ref, c_ref, su_ref, sd_ref, cq_ref, lat_ref, kr_ref):
    z = z_ref[...]
    cq_ref[...] = _rms_rows(z[:, :C_Q_RANK], gq_ref[...]).astype(cq_ref.dtype)
    lat_ref[...] = _rms_rows(z[:, C_Q_RANK:C_KR_COL], gkv_ref[...])
    kr = z[:, C_KR_COL:C_KR_COL + HEAD]
    kr_ref[...] = _rope(kr, c_ref[...], su_ref[...], sd_ref[...], C_ROPE_DIM // 2)


def mla_prep(z2d, gq, gkv, tables, *, seq_len, tt):
    m = z2d.shape[0]
    nt = seq_len // tt
    row = lambda i: (i, 0)
    fixed = lambda i: (0, 0)
    tab = pl.BlockSpec((tt, HEAD), lambda i: (i % nt, 0))
    return pl.pallas_call(
        _mla_prep_kernel,
        grid=(m // tt,),
        in_specs=[pl.BlockSpec((tt, C_MEM_COL), row), pl.BlockSpec((1, C_Q_RANK), fixed),
                  pl.BlockSpec((1, C_KV_RANK), fixed), tab, tab, tab],
        out_specs=[pl.BlockSpec((tt, C_Q_RANK), row), pl.BlockSpec((tt, C_KV_RANK), row),
                   pl.BlockSpec((tt, HEAD), row)],
        out_shape=[jax.ShapeDtypeStruct((m, C_Q_RANK), BF16),
                   jax.ShapeDtypeStruct((m, C_KV_RANK), F32),
                   jax.ShapeDtypeStruct((m, HEAD), F32)],
        compiler_params=_params(1),
        name="mla_prep",
    )(z2d, gq, gkv, *tables)


def _nt_dot(a, b):
    return lax.dot_general(a, b, (((1,), (1,)), ((), ())), preferred_element_type=F32)


def _lanes(x, n):
    return x if n == HEAD else jnp.concatenate([x] * (n // HEAD), axis=1)


def _chunk_mask(tq, tk, q_start, k_start):
    qpos = q_start + lax.broadcasted_iota(jnp.int32, (tq, tk), 0)
    kpos = k_start + lax.broadcasted_iota(jnp.int32, (tq, tk), 1)
    return jnp.right_shift(kpos, CHUNK_SHIFT) <= jnp.right_shift(qpos, CHUNK_SHIFT)


def _softmax_steps(scores, values, stats):
    probs, alphas = [], []
    for s, (m_ref, l_ref, _) in zip(scores, stats):
        tk = s.shape[1]
        m_old = m_ref[...]
        m_new = jnp.maximum(m_old, jnp.max(s, axis=-1, keepdims=True))
        alpha = jnp.exp2(m_old - m_new)
        p = jnp.exp2(s - _lanes(m_new, tk))
        psum = p[:, :HEAD]
        for c in range(1, tk // HEAD):
            psum = psum + p[:, c * HEAD:(c + 1) * HEAD]
        l_ref[...] = alpha * l_ref[...] + psum
        m_ref[...] = m_new
        probs.append(p.astype(BF16))
        alphas.append(alpha)
    for p, v, alpha, (_, _, acc_ref) in zip(probs, values, alphas, stats):
        acc_ref[...] = (_lanes(alpha, acc_ref.shape[-1]) * acc_ref[...]
                        + jnp.dot(p, v, preferred_element_type=F32))


def _softmax_scratch(n_chains, tq, dv):
    return [pltpu.VMEM((tq, HEAD), F32), pltpu.VMEM((tq, HEAD), F32), pltpu.VMEM((tq, dv), F32)] * n_chains


def _softmax_stats(scratch_refs):
    stats = [tuple(scratch_refs[3 * c:3 * c + 3]) for c in range(len(scratch_refs) // 3)]
    for m_ref, l_ref, acc_ref in stats:
        m_ref[...] = jnp.full_like(m_ref, NEG_INF)
        l_ref[...] = jnp.zeros_like(l_ref)
        acc_ref[...] = jnp.zeros_like(acc_ref)
    return stats


def _softmax_result(stat):
    _, l_ref, acc_ref = stat
    return acc_ref[...] / jnp.sum(l_ref[...], axis=-1, keepdims=True)


def _causal_key_blocks(block, q_start, tq):
    n_wide = q_start // ATT_WIDE

    def body(j, carry):
        block(pl.multiple_of(j * ATT_WIDE, ATT_WIDE), ATT_WIDE, False)
        return carry

    lax.fori_loop(0, n_wide, body, 0)
    if tq > ATT_BLK:
        block(pl.multiple_of(q_start, ATT_WIDE), ATT_WIDE, True)
        return
    rest = n_wide * ATT_WIDE

    @pl.when(q_start - rest >= ATT_BLK)
    def _():
        block(pl.multiple_of(rest, ATT_BLK), ATT_BLK, False)

    block(pl.multiple_of(q_start, ATT_BLK), ATT_BLK, True)


def _check_tiling(t, tq, q_off, s_len):
    assert t % tq == 0 and q_off % ATT_BLK == 0 and s_len % ATT_BLK == 0
    assert tq in (ATT_BLK, ATT_WIDE) or (t == tq and tq <= ATT_BLK), "own chunks must sit in one masked block"
    assert tq <= ATT_BLK or (q_off % ATT_WIDE == 0 and s_len % ATT_WIDE == 0)
    assert q_off + t <= s_len


def _diff_attn_kernel(q_ref, k_ref, v_ref, c_ref, su_ref, sd_ref, lam_ref, g_ref, o_ref,
                      *scratch_refs, tq, q_off, lam_init, heads):
    i = pl.program_id(2)
    q_start = q_off + i * tq
    scale = HEAD ** -0.5 * LOG2E
    tabs = (c_ref[...], su_ref[...], sd_ref[...])
    n_slots = 2 * heads
    qs = [(_rope(q_ref[0, :, sl * HEAD:(sl + 1) * HEAD].astype(F32), *tabs, A_ROT_DIM // 2) * scale).astype(BF16)
          for sl in range(n_slots)]
    stats = _softmax_stats(scratch_refs)

    def block(start, width, masked):
        rows = pl.ds(start, width)
        scores = [_nt_dot(qs[sl], k_ref[0, rows, sl * HEAD:(sl + 1) * HEAD]) for sl in range(n_slots)]
        if masked:
            mask = _chunk_mask(tq, width, q_start, start)
            scores = [jnp.where(mask, s, NEG_INF) for s in scores]
        values = [v_ref[0, rows, (sl // 2) * 2 * HEAD:(sl // 2 + 1) * 2 * HEAD] for sl in range(n_slots)]
        _softmax_steps(scores, values, stats)

    _causal_key_blocks(block, q_start, tq)

    lam_v = lam_ref[...]
    dots = jnp.sum(lam_v[0:2] * lam_v[2:4], axis=-1, keepdims=True)
    lam = jnp.exp(dots[0:1]) - jnp.exp(dots[1:2]) + lam_init
    for g in range(heads):
        o = _softmax_result(stats[2 * g]) - lam * _softmax_result(stats[2 * g + 1])
        o_ref[0, :, g * 2 * HEAD:(g + 1) * 2 * HEAD] = (
            _rms_rows(o, g_ref[...]) * (1.0 - lam_init)).astype(o_ref.dtype)


def diff_attention(q_src, k_rows, v_src, v_col0, q_tables, lam_vecs, norm_g, *, tq, q_off, lam_init, heads):
    bsz, t, _ = q_src.shape
    s_len = k_rows.shape[1]
    _check_tiling(t, tq, q_off, s_len)
    w = heads * 2 * HEAD
    assert A_HEADS % heads == 0 and v_col0 % w == 0
    vc = v_col0 // w
    tab = pl.BlockSpec((tq, HEAD), lambda b, h, i: (i, 0))
    kern = functools.partial(_diff_attn_kernel, tq=tq, q_off=q_off, lam_init=lam_init, heads=heads)
    return pl.pallas_call(
        kern,
        grid=(bsz, A_HEADS // heads, t // tq),
        in_specs=[pl.BlockSpec((1, tq, w), lambda b, h, i: (b, i, h)),
                  pl.BlockSpec((1, s_len, w), lambda b, h, i: (b, 0, h)),
                  pl.BlockSpec((1, s_len, w), lambda b, h, i: (b, 0, vc + h)),
                  tab, tab, tab,
                  pl.BlockSpec((4, HEAD), lambda b, h, i: (0, 0)),
                  pl.BlockSpec((1, 2 * HEAD), lambda b, h, i: (0, 0))],
        out_specs=pl.BlockSpec((1, tq, w), lambda b, h, i: (b, i, h)),
        out_shape=jax.ShapeDtypeStruct((bsz, t, A_HEADS * 2 * HEAD), BF16),
        scratch_shapes=_softmax_scratch(2 * heads, tq, 2 * HEAD),
        compiler_params=_params(3),
        name="diff_attention",
    )(q_src, k_rows, v_src, *q_tables, lam_vecs, norm_g)


def _mla_attn_kernel(qn_ref, qr_ref, kv_ref, kr_ref, c_ref, su_ref, sd_ref, o_ref,
                     *scratch_refs, tq, q_off, heads):
    i = pl.program_id(2)
    q_start = q_off + i * tq
    scale = (C_NOPE_DIM + C_ROPE_DIM) ** -0.5 * LOG2E
    tabs = (c_ref[...], su_ref[...], sd_ref[...])
    qn = [(qn_ref[0, :, g * HEAD:(g + 1) * HEAD].astype(F32) * scale).astype(BF16) for g in range(heads)]
    qr = [(_rope(qr_ref[0, :, g * HEAD:(g + 1) * HEAD].astype(F32), *tabs, C_ROPE_DIM // 2) * scale).astype(BF16)
          for g in range(heads)]
    qs = [jnp.concatenate([qn[g], qr[g]], axis=1) for g in range(heads)]
    stats = _softmax_stats(scratch_refs)

    def block(start, width, masked):
        rows = pl.ds(start, width)
        kr = kr_ref[0, rows, :]
        scores = [_nt_dot(qs[g], jnp.concatenate([kv_ref[0, rows, 2 * g * HEAD:(2 * g + 1) * HEAD], kr], axis=1))
                  for g in range(heads)]
        if masked:
            mask = _chunk_mask(tq, width, q_start, start)
            scores = [jnp.where(mask, s, NEG_INF) for s in scores]
        values = [kv_ref[0, rows, (2 * g + 1) * HEAD:(2 * g + 2) * HEAD] for g in range(heads)]
        _softmax_steps(scores, values, stats)

    _causal_key_blocks(block, q_start, tq)
    for g in range(heads):
        o_ref[0, :, g * HEAD:(g + 1) * HEAD] = _softmax_result(stats[g]).astype(o_ref.dtype)


def mla_attention(q, kv, kr, q_tables, *, tq, q_off, heads):
    bsz, t, _ = q.shape
    s_len = kv.shape[1]
    _check_tiling(t, tq, q_off, s_len)
    assert C_HEADS % heads == 0
    w = heads * HEAD
    n_groups = C_HEADS // heads
    tab = pl.BlockSpec((tq, HEAD), lambda b, h, i: (i, 0))
    kern = functools.partial(_mla_attn_kernel, tq=tq, q_off=q_off, heads=heads)
    return pl.pallas_call(
        kern,
        grid=(bsz, n_groups, t // tq),
        in_specs=[pl.BlockSpec((1, tq, w), lambda b, h, i: (b, i, h)),
                  pl.BlockSpec((1, tq, w), lambda b, h, i: (b, i, n_groups + h)),
                  pl.BlockSpec((1, s_len, 2 * w), lambda b, h, i: (b, 0, h)),
                  pl.BlockSpec((1, s_len, HEAD), lambda b, h, i: (b, 0, 0)),
                  tab, tab, tab],
        out_specs=pl.BlockSpec((1, tq, w), lambda b, h, i: (b, i, h)),
        out_shape=jax.ShapeDtypeStruct((bsz, t, C_HEADS * HEAD), BF16),
        scratch_shapes=_softmax_scratch(heads, tq, HEAD),
        compiler_params=_params(3),
        name="mla_attention",
    )(q, q, kv, kr, *q_tables)


def _softmax_rows(scores):
    probs, sums = [], []
    for s in scores:
        p = jnp.exp2(s - jnp.max(s, axis=-1, keepdims=True))
        sums.append(jnp.sum(p, axis=-1, keepdims=True))
        probs.append(p.astype(BF16))
    return probs, sums


def _mem_attn_kernel(q_ref, k_ref, v_ref, o_ref):
    col = lambda g: slice(g * HEAD, (g + 1) * HEAD)
    scores = [_nt_dot((q_ref[0, :, col(g)].astype(F32) * (HEAD ** -0.5 * LOG2E)).astype(BF16),
                      k_ref[0, :, col(g)].astype(BF16)) for g in range(MEM_HEADS)]
    probs, sums = _softmax_rows(scores)
    for g in range(MEM_HEADS):
        o = jnp.dot(probs[g], v_ref[0, :, col(g)].astype(BF16), preferred_element_type=F32)
        o_ref[0, :, col(g)] = (o / sums[g]).astype(o_ref.dtype)


def memory_attention(q_src, q_col0, mem_k, mem_v, k_col0, v_col0, *, tq):
    bsz, t, _ = q_src.shape
    n_mem = mem_k.shape[1]
    assert q_col0 % MEM_WIDTH == 0 and k_col0 % MEM_WIDTH == 0 and v_col0 % MEM_WIDTH == 0
    qc, kc, vc = q_col0 // MEM_WIDTH, k_col0 // MEM_WIDTH, v_col0 // MEM_WIDTH
    return pl.pallas_call(
        _mem_attn_kernel,
        grid=(bsz, t // tq),
        in_specs=[pl.BlockSpec((1, tq, MEM_WIDTH), lambda b, i: (b, i, qc)),
                  pl.BlockSpec((1, n_mem, MEM_WIDTH), lambda b, i: (b, 0, kc)),
                  pl.BlockSpec((1, n_mem, MEM_WIDTH), lambda b, i: (b, 0, vc))],
        out_specs=pl.BlockSpec((1, tq, MEM_WIDTH), lambda b, i: (b, i, 0)),
        out_shape=jax.ShapeDtypeStruct((bsz, t, MEM_WIDTH), BF16),
        compiler_params=_params(2),
        name="memory_attention",
    )(q_src, mem_k, mem_v)


def _band_attn_kernel(q_ref, k_ref, v_ref, e_ref, o_ref, bias_ref, *, front_pad, heads):
    i = pl.program_id(2)

    @pl.when(i == 0)
    def _():
        for g in range(heads):
            e = jnp.broadcast_to(e_ref[g] * LOG2E, (BAND_TQ, BAND_TABLE))
            bias_ref[g] = pltpu.roll(e, 0, 1, stride=1, stride_axis=0)[:, :BAND_KEYS]

    rows = pl.ds(pl.multiple_of(i * BAND_TQ, BAND_TQ), BAND_KEYS)
    qq = lax.broadcasted_iota(jnp.int32, (BAND_TQ, BAND_KEYS), 0)
    kk = lax.broadcasted_iota(jnp.int32, (BAND_TQ, BAND_KEYS), 1)
    lo = jnp.maximum(front_pad - i * BAND_TQ, jnp.where(qq < CHUNK, 0, CHUNK))
    hi = jnp.where(qq < CHUNK, BAND_KEYS - CHUNK, BAND_KEYS)
    mask = (kk >= lo) & (kk < hi)
    col = lambda g: slice(g * HEAD, (g + 1) * HEAD)
    scores = [_nt_dot((q_ref[0, :, col(g)].astype(F32) * (HEAD ** -0.5 * LOG2E)).astype(BF16),
                      k_ref[0, rows, col(g)]) for g in range(heads)]
    scores = [jnp.where(mask, s + bias_ref[g], NEG_INF) for g, s in enumerate(scores)]
    probs, sums = _softmax_rows(scores)
    for g in range(heads):
        o = jnp.dot(probs[g], v_ref[0, rows, col(g)], preferred_element_type=F32)
        o_ref[0, :, col(g)] = (o / sums[g]).astype(o_ref.dtype)


def band_attention(q_src, k_pad, v_pad, bias_tab, *, front_pad, heads):
    bsz, t, _ = q_src.shape
    s_len = k_pad.shape[1]
    assert t % BAND_TQ == 0 and s_len == t + B_WINDOW and B_HEADS % heads == 0
    w = heads * HEAD
    kern = functools.partial(_band_attn_kernel, front_pad=front_pad, heads=heads)
    return pl.pallas_call(
        kern,
        grid=(bsz, B_HEADS // heads, t // BAND_TQ),
        in_specs=[pl.BlockSpec((1, BAND_TQ, w), lambda b, h, i: (b, i, h)),
                  pl.BlockSpec((1, s_len, w), lambda b, h, i: (b, 0, h)),
                  pl.BlockSpec((1, s_len, w), lambda b, h, i: (b, 0, h)),
                  pl.BlockSpec((heads, 1, BAND_TABLE), lambda b, h, i: (h, 0, 0))],
        out_specs=pl.BlockSpec((1, BAND_TQ, w), lambda b, h, i: (b, i, h)),
        out_shape=jax.ShapeDtypeStruct((bsz, t, B_WIDTH), BF16),
        scratch_shapes=[pltpu.VMEM((heads, BAND_TQ, BAND_KEYS), F32)],
        compiler_params=_params(3),
        name="band_attention",
    )(q_src, k_pad, v_pad, bias_tab)


def _band_bias_table(rel_bias):
    c = jnp.arange(BAND_TABLE)
    d = jnp.where(c <= BAND_KEYS, c, c - BAND_TABLE)
    idx = jnp.clip(B_WINDOW - d, -B_REL_CLIP, B_REL_CLIP) + B_REL_CLIP
    return rel_bias[:, idx][:, None, :]


def _stick_attn_kernel(q_ref, k_ref, v_ref, o_ref, tri_ref, *state_refs, tq, q_off, heads):
    acc_refs, run_refs = state_refs[:heads], state_refs[heads:]
    i = pl.program_id(2)
    q_start = q_off + i * tq
    tk = ATT_BLK

    @pl.when((pl.program_id(0) == 0) & (pl.program_id(1) == 0) & (i == 0))
    def _():
        r = lax.broadcasted_iota(jnp.int32, (2 * tk, tk), 0)
        c = lax.broadcasted_iota(jnp.int32, (2 * tk, tk), 1)
        tri_ref[...] = jnp.where(jnp.where(r >= tk, r - tk, r) > c, 1.0, 0.0).astype(BF16)

    qs = [(q_ref[0, :, g * HEAD:(g + 1) * HEAD].astype(F32) * (HEAD ** -0.5 * LOG2E)).astype(BF16)
          for g in range(heads)]
    for ref in state_refs:
        ref[...] = jnp.zeros_like(ref)

    def block(start, masked):
        rows = pl.ds(start, tk)
        if masked:
            qpos = q_start + lax.broadcasted_iota(jnp.int32, (tq, tk), 0)
            kpos = start + lax.broadcasted_iota(jnp.int32, (tq, tk), 1)
            allowed = kpos < qpos
        tri = tri_ref[...]
        col = lambda g: slice(g * HEAD, (g + 1) * HEAD)
        zs = [_nt_dot(qs[g], k_ref[0, rows, col(g)]) for g in range(heads)]
        log_betas, log_1m_betas = [], []
        for z in zs:
            log_beta = jnp.minimum(z, 0.0) - jnp.log2(1.0 + jnp.exp2(-jnp.abs(z)))
            log_1m_beta = log_beta - z
            log_betas.append(log_beta)
            log_1m_betas.append(jnp.where(allowed, log_1m_beta, 0.0) if masked else log_1m_beta)
        tails = []
        for g, log_1m_beta in enumerate(log_1m_betas):
            hi = log_1m_beta.astype(BF16)
            lo = (log_1m_beta - hi.astype(F32)).astype(BF16)
            tails.append(jnp.dot(jnp.concatenate([hi, lo], axis=1), tri, preferred_element_type=F32)
                         + run_refs[g][...])
        for g in range(heads):
            a = jnp.exp2(log_betas[g] + tails[g])
            if masked:
                a = jnp.where(allowed, a, 0.0)
            acc_refs[g][...] += jnp.dot(a.astype(BF16), v_ref[0, rows, col(g)], preferred_element_type=F32)
            run_refs[g][...] += jnp.sum(log_1m_betas[g], axis=-1, keepdims=True)

    block(pl.multiple_of(q_start, tk), True)
    n_before = q_start // tk

    def any_stick_left():
        top = run_refs[0][...]
        for g in range(1, heads):
            top = jnp.maximum(top, run_refs[g][...])
        return (jnp.max(top) > STICK_DEAD_BITS).astype(jnp.int32)

    def more(carry):
        step, live = carry
        return (step < n_before) & (live > 0)

    def body(carry):
        step, _ = carry
        block(pl.multiple_of((n_before - 1 - step) * tk, tk), False)
        return step + 1, any_stick_left()

    lax.while_loop(more, body, (jnp.int32(0), any_stick_left()))
    for g in range(heads):
        o_ref[0, :, g * HEAD:(g + 1) * HEAD] = acc_refs[g][...].astype(o_ref.dtype)


def stick_attention(src, q_col0, k_rows, k_col0, v_rows, v_col0, *, tq, q_off, heads):
    bsz, t, _ = src.shape
    s_len = k_rows.shape[1]
    _check_tiling(t, tq, q_off, s_len)
    w = heads * HEAD
    assert D_HEADS % heads == 0 and q_col0 % w == 0 and k_col0 % w == 0 and v_col0 % w == 0
    qc, kc, vc = q_col0 // w, k_col0 // w, v_col0 // w
    kern = functools.partial(_stick_attn_kernel, tq=tq, q_off=q_off, heads=heads)
    return pl.pallas_call(
        kern,
        grid=(bsz, D_HEADS // heads, t // tq),
        in_specs=[pl.BlockSpec((1, tq, w), lambda b, h, i: (b, i, qc + h)),
                  pl.BlockSpec((1, s_len, w), lambda b, h, i: (b, 0, kc + h)),
                  pl.BlockSpec((1, s_len, w), lambda b, h, i: (b, 0, vc + h))],
        out_specs=pl.BlockSpec((1, tq, w), lambda b, h, i: (b, i, h)),
        out_shape=jax.ShapeDtypeStruct((bsz, t, D_WIDTH), BF16),
        scratch_shapes=([pltpu.VMEM((2 * ATT_BLK, ATT_BLK), BF16)] + [pltpu.VMEM((tq, HEAD), F32)] * heads
                        + [pltpu.VMEM((tq, 1), F32)] * heads),
        compiler_params=_params(3),
        name="stick_attention",
    )(src, k_rows, v_rows)


def _pad_rows(a, front, back):
    return jnp.pad(a, ((0, 0), (front, back), (0, 0)))


def _round_up(n, mult):
    return (n + mult - 1) // mult * mult


def _with_past(past, new):
    bsz, p_len = past.shape[0], past.shape[1]
    rows = jnp.concatenate([past.reshape(bsz, p_len, new.shape[2]).astype(BF16), new], axis=1)
    return _pad_rows(rows, 0, _round_up(rows.shape[1], ATT_BLK) - rows.shape[1])


def _mixer_diff(kf, v_rows, zb, pos, past, params, lam_init):
    lq1, lk1, lq2, lk2, norm_g = params
    bsz, t, _ = kf.shape
    tables = _rope_tables(pos, A_ROT_DIM // 2, ROPE_THETA)
    k_rows, k_rows_b = rope_heads(kf, 0, 2 * A_HEADS, tables, A_ROT_DIM // 2, tt=min(t, 512))
    v_col0 = 2 * A_QK_WIDTH
    lam_vecs = jnp.stack([lq1, lq2, lk1, lk2]).astype(F32)
    g = norm_g.reshape(1, 2 * HEAD).astype(F32)
    if past is None:
        o = diff_attention(zb, k_rows_b, zb, v_col0, tables, lam_vecs, g, tq=ATT_WIDE, q_off=0,
                           lam_init=lam_init, heads=2)
    else:
        k_all = _with_past(past[0], k_rows_b)
        v_all = _with_past(past[1], zb[:, :, v_col0:v_col0 + MIX_WIDTH])
        o = diff_attention(zb, k_all, v_all, 0, tables, lam_vecs, g, tq=t, q_off=past[0].shape[1],
                           lam_init=lam_init, heads=3)
    shape = (bsz, t, A_HEADS, 2 * HEAD)
    return o, (k_rows.reshape(shape), v_rows.reshape(shape))


def _mixer_band(k, v, zb, pos, past, params):
    (rel_bias,) = params
    bsz, t, _ = k.shape
    kb = zb[:, :, B_WIDTH:2 * B_WIDTH]
    vb = zb[:, :, 2 * B_WIDTH:3 * B_WIDTH]
    bias_tab = _band_bias_table(rel_bias.astype(F32))
    shape = (bsz, -1, B_HEADS, HEAD)
    if past is None:
        o = band_attention(zb, _pad_rows(kb, B_WINDOW, 0), _pad_rows(vb, B_WINDOW, 0), bias_tab,
                           front_pad=B_WINDOW, heads=4)
        keep = min(B_WINDOW, t)
        state = (k[:, t - keep:].reshape(shape), v[:, t - keep:].reshape(shape))
    else:
        buf_len = past[0].shape[1]
        assert t == CHUNK and buf_len == B_WINDOW
        k_all = jnp.concatenate([past[0].reshape(bsz, buf_len, B_WIDTH), k], axis=1)
        v_all = jnp.concatenate([past[1].reshape(bsz, buf_len, B_WIDTH), v], axis=1)
        q_pad = _pad_rows(zb[:, :, :B_WIDTH], BAND_TQ - t, 0)
        o = band_attention(q_pad, _pad_rows(k_all.astype(BF16), CHUNK, 0), _pad_rows(v_all.astype(BF16), CHUNK, 0),
                           bias_tab, front_pad=CHUNK, heads=6)[:, BAND_TQ - t:]
        state = (k_all[:, t:].reshape(shape), v_all[:, t:].reshape(shape))
    return o, state


def _mixer_mla(z, pos, past, params, w_uq, w_ukv):
    q_norm_g, kv_norm_g = params
    bsz, t, _ = z.shape
    half = C_ROPE_DIM // 2
    tables = _rope_tables(pos, half, C_ROPE_THETA)
    cq, latent, kr = mla_prep(z.reshape(bsz * t, z.shape[-1]), q_norm_g.reshape(1, -1).astype(F32),
                              kv_norm_g.reshape(1, -1).astype(F32), tables, seq_len=t, tt=min(t, 512))
    q = matmul(cq, w_uq, tm=min(bsz * t, 1024), tn=1024, out_dtypes=(BF16,)).reshape(bsz, t, -1)
    latent = latent.reshape(bsz, t, C_KV_RANK)
    kr = kr.reshape(bsz, t, HEAD)
    if past is None:
        lat_all, kr_all, q_off, tq, heads = latent, kr.astype(BF16), 0, ATT_WIDE, 4
    else:
        q_off, tq, heads = past[0].shape[1], t, 6
        back = _round_up(q_off + t, ATT_BLK) - q_off - t
        lat_all = _pad_rows(jnp.concatenate([past[0], latent], axis=1), 0, back)
        kr_past = jnp.pad(past[1], ((0, 0), (0, 0), (0, HEAD - C_ROPE_DIM)))
        kr_all = _pad_rows(jnp.concatenate([kr_past, kr], axis=1), 0, back).astype(BF16)
    s_len = lat_all.shape[1]
    kv = matmul(lat_all.reshape(bsz * s_len, C_KV_RANK), w_ukv, tm=_pick(bsz * s_len, (1024, 512, 256)),
                tn=1024, out_dtypes=(BF16,)).reshape(bsz, s_len, -1)
    o = mla_attention(q, kv, kr_all, tables, tq=tq, q_off=q_off, heads=heads)
    return o, (latent, kr[:, :, :C_ROPE_DIM])


def _mixer_stick(k, v, zb, pos, past):
    bsz, t, _ = k.shape
    if past is None:
        o = stick_attention(zb, 0, zb, D_WIDTH, zb, 2 * D_WIDTH, tq=ATT_TQ, q_off=0, heads=4)
    else:
        k_all = _with_past(past[0], zb[:, :, D_WIDTH:2 * D_WIDTH])
        v_all = _with_past(past[1], zb[:, :, 2 * D_WIDTH:3 * D_WIDTH])
        o = stick_attention(zb, 0, k_all, 0, v_all, 0, tq=t, q_off=past[0].shape[1], heads=6)
    shape = (bsz, t, D_HEADS, HEAD)
    return o, (k.reshape(shape), v.reshape(shape))


def _trunk_layer(layer, x, pos, past, mem_kv, conv_state, mix_params, w):
    bsz, t, _ = x.shape
    m = bsz * t
    mixer = layer % N_MIXERS
    x2d = x.reshape(m, D_MODEL)
    z, zb = matmul(x2d, w["w_in"], tm=min(m, 1024), tn=1024, out_dtypes=(F32, BF16))
    z = z.reshape(bsz, t, -1)
    zb = zb.reshape(bsz, t, -1)
    zf = [z] if mixer == 2 else [z[:, :, MIX_WIDTH:2 * MIX_WIDTH], z[:, :, 2 * MIX_WIDTH:3 * MIX_WIDTH]]
    if mixer == 0:
        o_mix, state = _mixer_diff(*zf, zb, pos, past, mix_params, 0.8 - 0.6 * math.exp(-0.3 * layer))
    elif mixer == 1:
        o_mix, state = _mixer_band(*zf, zb, pos, past, mix_params)
    elif mixer == 2:
        o_mix, state = _mixer_mla(*zf, pos, past, mix_params, w["w_uq"], w["w_ukv"])
    else:
        o_mix, state = _mixer_stick(*zf, zb, pos, past)
    q_mem_col0 = C_MEM_COL if mixer == 2 else 3 * MIX_WIDTH
    o_mem = memory_attention(zb, q_mem_col0, mem_kv[0], mem_kv[1], mem_kv[2], mem_kv[3], tq=min(t, 512))
    x1 = proj_ln([o_mix.reshape(m, MIX_WIDTH), o_mem.reshape(m, MEM_WIDTH)], w["w_o"], x2d,
                 w["ln1_g"], w["ln1_b"], tm=512, name="out_proj_ln")
    g = ffn_up(x1, w["w_up"], w["conv_w"], w["conv_b"], conv_state, seq_len=t, tm=512, tn=512)
    x2 = proj_ln([g], w["w_down"], x1, w["ln2_g"], w["ln2_b"], tm=256, name="down_proj_ln")
    return x2.reshape(bsz, t, D_MODEL), state, x1.reshape(bsz, t, D_MODEL)


def _reorder_w_in_c(w):
    a, b = C_KR_COL, C_KR_COL + C_ROPE_DIM
    pad = jnp.zeros((w.shape[0], C_MEM_COL - b), w.dtype)
    return jnp.concatenate([w[:, :b], pad, w[:, b:]], axis=1)


def _reorder_w_uq(w):
    w = w.reshape(C_Q_RANK, C_HEADS, C_NOPE_DIM + C_ROPE_DIM)
    nope = w[:, :, :C_NOPE_DIM].reshape(C_Q_RANK, C_HEADS * HEAD)
    rope = jnp.pad(w[:, :, C_NOPE_DIM:], ((0, 0), (0, 0), (0, HEAD - C_ROPE_DIM)))
    return jnp.concatenate([nope, rope.reshape(C_Q_RANK, C_HEADS * HEAD)], axis=1)


def kernel(x_prompt, x_sample, mem_prompt, cache_a_k, cache_a_v, cache_b_k, cache_b_v, cache_c_latent, cache_c_krope, cache_d_k, cache_d_v, cache_mem_k, cache_mem_v, state_ffn_conv, w_in_a, w_in_b, w_in_c, w_in_d, diff_lambda_q1, diff_lambda_k1, diff_lambda_q2, diff_lambda_k2, diff_norm_g, band_rel_bias, mla_q_norm_g, mla_kv_norm_g, mla_w_uq, mla_w_ukv, w_mem_kv, w_o, ln1_g, ln1_b, w_up, conv_ffn_w, conv_ffn_b, w_down, ln2_g, ln2_b):
    n_p, t_p, _ = x_prompt.shape
    n_s, t_s, _ = x_sample.shape
    past_len = cache_d_k.shape[2]
    pos_p = jnp.arange(t_p)
    pos_s = past_len + jnp.arange(t_s)
    caches_by_type = ((cache_a_k, cache_a_v), (cache_b_k, cache_b_v),
                      (cache_c_latent, cache_c_krope), (cache_d_k, cache_d_v))
    params_by_type = ((diff_lambda_q1, diff_lambda_k1, diff_lambda_q2, diff_lambda_k2, diff_norm_g),
                      (band_rel_bias,), (mla_q_norm_g, mla_kv_norm_g), ())
    states_p = [([], []) for _ in range(N_MIXERS)]
    states_s = [([], []) for _ in range(N_MIXERS)]
    mem_k_p, mem_v_p, conv_p, conv_s = [], [], [], []
    n_mem = mem_prompt.shape[1]
    mem2d = mem_prompt.reshape(n_p * n_mem, D_MODEL)
    zero_state = jnp.zeros((n_p, 2, 2 * D_FF), F32)
    x_p, x_s = x_prompt, x_sample
    w_o_b, w_up_b, w_down_b, w_mem_kv_b = (a.astype(BF16) for a in (w_o, w_up, w_down, w_mem_kv))
    w_in_bf = {0: w_in_a.astype(BF16), 1: w_in_b.astype(BF16), 3: w_in_d.astype(BF16)}
    for i in range(DEPTH):
        mixer, j = i % N_MIXERS, i // N_MIXERS
        w = {
            "w_in": (_reorder_w_in_c(w_in_c[j]).astype(BF16) if mixer == 2 else (w_in_bf[mixer], j)),
            "w_o": (w_o_b, i),
            "ln1_g": ln1_g[i].reshape(1, -1), "ln1_b": ln1_b[i].reshape(1, -1),
            "w_up": (w_up_b, i),
            "conv_w": conv_ffn_w[i], "conv_b": conv_ffn_b[i].reshape(1, -1),
            "w_down": (w_down_b, i),
            "ln2_g": ln2_g[i].reshape(1, -1), "ln2_b": ln2_b[i].reshape(1, -1),
        }
        if mixer == 2:
            w["w_uq"] = _reorder_w_uq(mla_w_uq[j]).astype(BF16)
            w["w_ukv"] = mla_w_ukv[j].astype(BF16)
        mix_params = tuple(p[j] for p in params_by_type[mixer])
        kv_mem, kv_mem_b = matmul(mem2d, (w_mem_kv_b, i), tm=n_p * n_mem, tn=512, out_dtypes=(F32, BF16))
        kv_mem = kv_mem.reshape(n_p, n_mem, 2 * MEM_WIDTH)
        kv_mem_b = kv_mem_b.reshape(n_p, n_mem, 2 * MEM_WIDTH)
        x_p, st_p, x1_p = _trunk_layer(i, x_p, pos_p, None, (kv_mem_b, kv_mem_b, 0, MEM_WIDTH), zero_state,
                                       mix_params, w)
        past = (caches_by_type[mixer][0][j], caches_by_type[mixer][1][j])
        mem_s = (cache_mem_k[i].reshape(n_s, n_mem, MEM_WIDTH), cache_mem_v[i].reshape(n_s, n_mem, MEM_WIDTH), 0, 0)
        x_s, st_s, x1_s = _trunk_layer(i, x_s, pos_s, past, mem_s, state_ffn_conv[i], mix_params, w)
        tails = jnp.concatenate([x1_p[:, t_p - 2:].reshape(2 * n_p, D_MODEL),
                                 x1_s[:, t_s - 2:].reshape(2 * n_s, D_MODEL)], axis=0)
        n_tail = tails.shape[0]
        tails = jnp.pad(tails, ((0, _round_up(n_tail, 16) - n_tail), (0, 0)))
        u_tail = matmul(tails, w["w_up"], tm=tails.shape[0], tn=1024)
        conv_p.append(u_tail[:2 * n_p].reshape(n_p, 2, 2 * D_FF))
        conv_s.append(u_tail[2 * n_p:n_tail].reshape(n_s, 2, 2 * D_FF))
        for a in range(2):
            states_p[mixer][a].append(st_p[a])
            states_s[mixer][a].append(st_s[a])
        mem_k_p.append(kv_mem[:, :, :MEM_WIDTH].reshape(n_p, n_mem, MEM_HEADS, HEAD))
        mem_v_p.append(kv_mem[:, :, MEM_WIDTH:].reshape(n_p, n_mem, MEM_HEADS, HEAD))
    outs = [x_p, x_s]
    for st in states_p:
        outs += [jnp.stack(st[0]), jnp.stack(st[1])]
    outs += [jnp.stack(mem_k_p), jnp.stack(mem_v_p), jnp.stack(conv_p)]
    for st in states_s:
        outs += [jnp.stack(st[0]), jnp.stack(st[1])]
    outs.append(jnp.stack(conv_s))
    return tuple(outs)
```
